```python
import math
import jax
import jax.numpy as jnp
from jax import lax
import numpy as np

D_MODEL = 1024
BATCH = 8
SEQ = 2048
DEPTH = 2
DEC_BATCH = 32
DEC_SEQ = 1
PAST_LEN = 16384
PAGE_SIZE = 128

HEAD_DIM = 64
H_A = 6
H_B = 4
H_C = 6
MIX_WIDTH = (H_A + H_B + H_C) * HEAD_DIM
MOBA_BLOCK = 256
MOBA_TOPK = 3
T5_BUCKETS = 32
T5_MAX_DIST = 128
RET_ROPE_BASE = 10000.0
Q_LORA = 256
KV_LORA = 128
D_NOPE = 64
D_ROPE = 32
MLA_ROPE_BASE = 10000.0
D_FF = 2816
CONV_W = 3
Q_BLOCK = 128
NORM_EPS = 1e-6
NEG_INF = -1e30
IN_WIDTHS = (H_A * HEAD_DIM, H_A * HEAD_DIM, H_A * HEAD_DIM,
             H_B * HEAD_DIM, H_B * HEAD_DIM, H_B * HEAD_DIM, H_B * HEAD_DIM,
             Q_LORA, KV_LORA, D_ROPE)
N_IN = sum(IN_WIDTHS)

kernel_name = 'hybrid_moba_retention_mla_convffn_step'


def rmsnorm(x, g):
    xf = x.astype(jnp.float32)
    y = xf * lax.rsqrt(jnp.mean(xf * xf, axis=-1, keepdims=True) + NORM_EPS)
    return (y * g.astype(jnp.float32)).astype(x.dtype)


def rope(x, pos, base):
    half = x.shape[-1] // 2
    inv = base ** (-jnp.arange(half, dtype=jnp.float32) / half)
    ang = pos.astype(jnp.float32)[:, None] * inv[None, :]
    cos = jnp.cos(ang)[:, None, :]
    sin = jnp.sin(ang)[:, None, :]
    xf = x.astype(jnp.float32)
    x1, x2 = xf[..., :half], xf[..., half:]
    return jnp.concatenate([x1 * cos - x2 * sin, x1 * sin + x2 * cos], axis=-1).astype(x.dtype)


def t5_bucket(dist):
    n = jnp.maximum(dist, 0)
    max_exact = T5_BUCKETS // 2
    nf = jnp.maximum(n, 1).astype(jnp.float32)
    large = max_exact + (jnp.log(nf / max_exact) / math.log(T5_MAX_DIST / max_exact)
                         * (T5_BUCKETS - max_exact)).astype(jnp.int32)
    large = jnp.minimum(large, T5_BUCKETS - 1)
    return jnp.where(n < max_exact, n, large)


def moba_attention(q, k, v, rel_bias, offset):
    B, S, H, dh = q.shape
    L = k.shape[1]
    f32 = jnp.float32
    nb = -(-L // MOBA_BLOCK)
    pad = nb * MOBA_BLOCK - L
    kb = jnp.pad(k, ((0, 0), (0, pad), (0, 0), (0, 0))).reshape(B, nb, MOBA_BLOCK, H, dh).transpose(0, 3, 1, 2, 4)
    vb = jnp.pad(v, ((0, 0), (0, pad), (0, 0), (0, 0))).reshape(B, nb, MOBA_BLOCK, H, dh).transpose(0, 3, 1, 2, 4)
    kmean = jnp.mean(kb.astype(f32), axis=3)
    pos = offset + jnp.arange(S)
    own = pos // MOBA_BLOCK
    gate = jnp.einsum('bshd,bhnd->bshn', q.astype(f32), kmean)
    eligible = jnp.arange(nb)[None, :] < own[:, None]
    gate = jnp.where(eligible[None, :, None, :], gate, NEG_INF)
    n_sel = min(MOBA_TOPK, nb)
    _, sel = lax.top_k(gate, n_sel)
    valid = sel < own[None, :, None, None]
    c = math.gcd(S, Q_BLOCK)
    nc = S // c
    q_items = q.reshape(B * nc, c, H, dh)
    sel_items = sel.reshape(B * nc, c, H, n_sel)
    valid_items = valid.reshape(B * nc, c, H, n_sel)
    b_items = jnp.repeat(jnp.arange(B), nc)
    p0_items = offset + (jnp.arange(B * nc) % nc) * c
    bias_t = rel_bias.astype(f32).T
    h_idx = jnp.arange(H)
    blk_off = jnp.arange(MOBA_BLOCK)
    scale = dh ** -0.5

    def attend(item):
        qc, sc, vc, b, p0 = item
        kbb = kb[b]
        vbb = vb[b]
        qf = qc.astype(f32) * scale
        qpos = p0 + jnp.arange(c)
        hsel = h_idx[None, :, None]
        ks = kbb[hsel, sc].astype(f32)
        vs = vbb[hsel, sc].astype(f32)
        kpos = sc[..., None] * MOBA_BLOCK + blk_off
        ls = jnp.einsum('chd,chknd->chkn', qf, ks)
        ls = ls + bias_t[h_idx[None, :, None, None], t5_bucket(qpos[:, None, None, None] - kpos)]
        ls = jnp.where(vc[..., None], ls, NEG_INF)
        ob = p0 // MOBA_BLOCK
        ko = lax.dynamic_index_in_dim(kbb, ob, axis=1, keepdims=False).astype(f32)
        vo = lax.dynamic_index_in_dim(vbb, ob, axis=1, keepdims=False).astype(f32)
        dist = qpos[:, None] - (ob * MOBA_BLOCK + blk_off)[None, :]
        lo = jnp.einsum('chd,hnd->chn', qf, ko) + bias_t[:, t5_bucket(dist)].transpose(1, 0, 2)
        lo = jnp.where((dist >= 0)[:, None, :], lo, NEG_INF)
        probs = jax.nn.softmax(jnp.concatenate([ls.reshape(c, H, n_sel * MOBA_BLOCK), lo], axis=-1), axis=-1)
        ps = probs[..., :n_sel * MOBA_BLOCK].reshape(c, H, n_sel, MOBA_BLOCK)
        po = probs[..., n_sel * MOBA_BLOCK:]
        out = jnp.einsum('chkn,chknd->chd', ps, vs) + jnp.einsum('chn,hnd->chd', po, vo)
        return out.astype(q.dtype)

    out = lax.map(attend, (q_items, sel_items, valid_items, b_items, p0_items))
    return out.reshape(B, S, H, dh)


def retention(q, k, v, state0):
    B, S, H, dk = q.shape
    dv = v.shape[-1]
    f32 = jnp.float32
    c = math.gcd(S, Q_BLOCK)
    nc = S // c
    log_g = jnp.log(1.0 - 2.0 ** (-5.0 - jnp.arange(H, dtype=f32)))
    i = jnp.arange(c, dtype=f32)
    diff = i[:, None] - i[None, :]
    intra = jnp.where(diff >= 0, jnp.exp(log_g[:, None, None] * jnp.maximum(diff, 0.0)), 0.0)
    into = jnp.exp(log_g[:, None] * (i + 1.0))[None, :, :, None]
    out_of = jnp.exp(log_g[:, None] * (c - 1.0 - i))[None, :, :, None]
    carry_decay = jnp.exp(log_g * c)[None, :, None, None]

    def chunks(t, d):
        return t.astype(f32).reshape(B, nc, c, H, d).transpose(1, 0, 3, 2, 4)

    qc, kc, vc = chunks(q, dk), chunks(k, dk), chunks(v, dv)

    def step(state, inp):
        qi, ki, vi = inp
        a = jnp.einsum('bhid,bhjd->bhij', qi, ki) * intra
        o = jnp.einsum('bhij,bhjv->bhiv', a, vi) + jnp.einsum('bhid,bhdv->bhiv', qi, state) * into
        state = state * carry_decay + jnp.einsum('bhjd,bhjv->bhdv', ki * out_of, vi)
        return state, o

    state, o = lax.scan(step, state0.astype(f32), (qc, kc, vc))
    return o.transpose(1, 0, 3, 2, 4).reshape(B, S, H, dv), state


def mla_attention(q_lat, q_rope, c_kv, k_rope, offset):
    B, S, H, R = q_lat.shape
    L = c_kv.shape[1]
    f32 = jnp.float32
    c = math.gcd(S, Q_BLOCK)
    nc = S // c
    scale = (D_NOPE + D_ROPE) ** -0.5
    ql = q_lat.astype(f32).reshape(B, nc, c, H, R).transpose(1, 0, 2, 3, 4) * scale
    qr = q_rope.astype(f32).reshape(B, nc, c, H, D_ROPE).transpose(1, 0, 2, 3, 4) * scale
    ckv = c_kv.astype(f32)
    kr = k_rope.astype(f32)
    kpos = jnp.arange(L)
    p0s = offset + jnp.arange(nc) * c

    def attend(item):
        qlc, qrc, p0 = item
        s = jnp.einsum('bchr,blr->bhcl', qlc, ckv) + jnp.einsum('bchd,bld->bhcl', qrc, kr)
        mask = kpos[None, :] <= (p0 + jnp.arange(c))[:, None]
        p = jax.nn.softmax(jnp.where(mask, s, NEG_INF), axis=-1)
        return jnp.einsum('bhcl,blr->bchr', p, ckv)

    o = lax.map(attend, (ql, qr, p0s))
    return o.transpose(1, 0, 2, 3, 4).reshape(B, S, H, R).astype(q_lat.dtype)


def layer(x, offset, past_k, past_v, past_mla, ret_state, conv_state,
          norm1_g, w_in, rel_bias, ret_norm_g, mla_q_norm_g, mla_kv_norm_g, w_uq, w_uk, w_uv, w_o,
          norm2_g, w_up, conv_w, conv_b, w_down):
    B, S, _ = x.shape
    pos = offset + jnp.arange(S)
    h = rmsnorm(x, norm1_g)
    splits = np.cumsum(IN_WIDTHS)[:-1].tolist()
    a_q, a_k, a_v, r_q, r_k, r_v, r_g, m_cq, m_ckv, m_kr = jnp.split(h @ w_in, splits, axis=-1)
    a_q = a_q.reshape(B, S, H_A, HEAD_DIM)
    a_k = a_k.reshape(B, S, H_A, HEAD_DIM)
    a_v = a_v.reshape(B, S, H_A, HEAD_DIM)
    k_all = a_k if past_k is None else jnp.concatenate([past_k.astype(a_k.dtype), a_k], axis=1)
    v_all = a_v if past_v is None else jnp.concatenate([past_v.astype(a_v.dtype), a_v], axis=1)
    o_a = moba_attention(a_q, k_all, v_all, rel_bias, offset).reshape(B, S, H_A * HEAD_DIM)
    rq = rope(r_q.reshape(B, S, H_B, HEAD_DIM), pos, RET_ROPE_BASE)
    rk = rope(r_k.reshape(B, S, H_B, HEAD_DIM), pos, RET_ROPE_BASE) * (HEAD_DIM ** -0.5)
    rv = r_v.reshape(B, S, H_B, HEAD_DIM)
    o_b, ret_new = retention(rq, rk, rv, ret_state)
    o_b = rmsnorm(o_b, ret_norm_g.reshape(H_B, HEAD_DIM)).reshape(B, S, H_B * HEAD_DIM).astype(x.dtype)
    o_b = o_b * jax.nn.silu(r_g)
    cq = (rmsnorm(m_cq, mla_q_norm_g) @ w_uq).reshape(B, S, H_C, D_NOPE + D_ROPE)
    q_nope = cq[..., :D_NOPE]
    q_rope = rope(cq[..., D_NOPE:], pos, MLA_ROPE_BASE)
    ckv = rmsnorm(m_ckv, mla_kv_norm_g)
    kr = rope(m_kr[:, :, None, :], pos, MLA_ROPE_BASE)[:, :, 0, :]
    mla_new = jnp.concatenate([ckv, kr], axis=-1)
    mla_all = mla_new if past_mla is None else jnp.concatenate([past_mla.astype(mla_new.dtype), mla_new], axis=1)
    q_lat = jnp.einsum('bshd,rhd->bshr', q_nope, w_uk)
    o_lat = mla_attention(q_lat, q_rope, mla_all[..., :KV_LORA], mla_all[..., KV_LORA:], offset)
    o_c = jnp.einsum('bshr,rhd->bshd', o_lat, w_uv).reshape(B, S, H_C * HEAD_DIM)
    x = x + jnp.concatenate([o_a, o_b, o_c], axis=-1) @ w_o
    u = rmsnorm(x, norm2_g) @ w_up
    up = jnp.concatenate([conv_state.astype(u.dtype), u], axis=1)
    conv = conv_b
    for i in range(CONV_W):
        conv = conv + conv_w[i] * up[:, i:i + S]
    g, val = jnp.split(conv, 2, axis=-1)
    x = x + (jax.nn.silu(g) * val) @ w_down
    conv_new = up[:, up.shape[1] - (CONV_W - 1):]
    return x, a_k, a_v, mla_new, ret_new.astype(x.dtype), conv_new


def setup_inputs(seed: int = 0) -> dict:
    key = jax.random.key(seed)
    ks = jax.random.split(key, 32)
    f32 = jnp.float32
    n_pages = PAST_LEN // PAGE_SIZE
    n_pool = (5 * DEC_BATCH * n_pages) // 4

    def nrm(k, shape, scale):
        return jax.random.normal(k, shape, f32) * scale

    def gain(k, shape):
        return 1.0 + nrm(k, shape, 0.02)

    page_table = jax.random.permutation(ks[7], n_pool)[:DEC_BATCH * n_pages].reshape(DEC_BATCH, n_pages).astype(jnp.int32)
    return {
        'x_prompt': nrm(ks[0], (BATCH, SEQ, D_MODEL), 1.0),
        'x_sample': nrm(ks[1], (DEC_BATCH, DEC_SEQ, D_MODEL), 1.0),
        'cache_moba_k': nrm(ks[2], (DEPTH, n_pool, PAGE_SIZE, H_A, HEAD_DIM), 1.0),
        'cache_moba_v': nrm(ks[3], (DEPTH, n_pool, PAGE_SIZE, H_A, HEAD_DIM), 1.0),
        'cache_mla': nrm(ks[4], (DEPTH, n_pool, PAGE_SIZE, KV_LORA + D_ROPE), 1.0),
        'state_ret': nrm(ks[5], (DEPTH, DEC_BATCH, H_B, HEAD_DIM, HEAD_DIM), 0.5),
        'state_conv': nrm(ks[6], (DEPTH, DEC_BATCH, CONV_W - 1, 2 * D_FF), 1.0),
        'page_table': page_table,
        'norm1_g': gain(ks[8], (DEPTH, D_MODEL)),
        'w_in': nrm(ks[9], (DEPTH, D_MODEL, N_IN), D_MODEL ** -0.5),
        'rel_bias': nrm(ks[10], (T5_BUCKETS, H_A), 0.5),
        'ret_norm_g': gain(ks[11], (DEPTH, H_B * HEAD_DIM)),
        'mla_q_norm_g': gain(ks[12], (DEPTH, Q_LORA)),
        'mla_kv_norm_g': gain(ks[13], (DEPTH, KV_LORA)),
        'w_uq': nrm(ks[14], (DEPTH, Q_LORA, H_C * (D_NOPE + D_ROPE)), Q_LORA ** -0.5),
        'w_uk': nrm(ks[15], (DEPTH, KV_LORA, H_C, D_NOPE), KV_LORA ** -0.5),
        'w_uv': nrm(ks[16], (DEPTH, KV_LORA, H_C, HEAD_DIM), KV_LORA ** -0.5),
        'w_o': nrm(ks[17], (DEPTH, MIX_WIDTH, D_MODEL), MIX_WIDTH ** -0.5),
        'norm2_g': gain(ks[18], (DEPTH, D_MODEL)),
        'w_up': nrm(ks[19], (DEPTH, D_MODEL, 2 * D_FF), D_MODEL ** -0.5),
        'conv_w': nrm(ks[20], (DEPTH, CONV_W, 2 * D_FF), CONV_W ** -0.5),
        'conv_b': nrm(ks[21], (DEPTH, 2 * D_FF), 0.01),
        'w_down': nrm(ks[22], (DEPTH, D_FF, D_MODEL), D_FF ** -0.5),
        'final_norm_g': gain(ks[23], (D_MODEL,)),
    }


def reference(x_prompt, x_sample, cache_moba_k, cache_moba_v, cache_mla, state_ret, state_conv, page_table,
              norm1_g, w_in, rel_bias, ret_norm_g, mla_q_norm_g, mla_kv_norm_g, w_uq, w_uk, w_uv, w_o,
              norm2_g, w_up, conv_w, conv_b, w_down, final_norm_g):
    n_seq_p = x_prompt.shape[0]
    n_seq_s = x_sample.shape[0]
    n_pages = page_table.shape[1]
    past_len = n_pages * cache_moba_k.shape[2]
    xp, xs = x_prompt, x_sample
    pk, pv, pm, pr, pc = [], [], [], [], []
    sk, sv, sm, sr, sc = [], [], [], [], []
    for l in range(DEPTH):
        w = (norm1_g[l], w_in[l], rel_bias, ret_norm_g[l], mla_q_norm_g[l], mla_kv_norm_g[l],
             w_uq[l], w_uk[l], w_uv[l], w_o[l], norm2_g[l], w_up[l], conv_w[l], conv_b[l], w_down[l])
        ret0 = jnp.zeros((n_seq_p, H_B, HEAD_DIM, HEAD_DIM), jnp.float32)
        conv0 = jnp.zeros((n_seq_p, CONV_W - 1, 2 * D_FF), xp.dtype)
        xp, k_new, v_new, m_new, r_new, c_new = layer(xp, 0, None, None, None, ret0, conv0, *w)
        pk.append(k_new); pv.append(v_new); pm.append(m_new); pr.append(r_new); pc.append(c_new)
        past_k = cache_moba_k[l][page_table].reshape(n_seq_s, past_len, H_A, HEAD_DIM)
        past_v = cache_moba_v[l][page_table].reshape(n_seq_s, past_len, H_A, HEAD_DIM)
        past_m = cache_mla[l][page_table].reshape(n_seq_s, past_len, KV_LORA + D_ROPE)
        xs, k_new, v_new, m_new, r_new, c_new = layer(xs, past_len, past_k, past_v, past_m,
                                                      state_ret[l], state_conv[l], *w)
        sk.append(k_new); sv.append(v_new); sm.append(m_new); sr.append(r_new); sc.append(c_new)
    y_prompt = rmsnorm(xp, final_norm_g)
    y_sample = rmsnorm(xs, final_norm_g)
    return (y_prompt, y_sample,
            jnp.stack(pk), jnp.stack(pv), jnp.stack(pm), jnp.stack(pr), jnp.stack(pc),
            jnp.stack(sk), jnp.stack(sv), jnp.stack(sm), jnp.stack(sr), jnp.stack(sc))
```

```python
import functools
import math

import jax
import jax.numpy as jnp
from jax import lax
from jax.experimental import pallas as pl
from jax.experimental.pallas import tpu as pltpu

F32 = jnp.float32
BF16 = jnp.bfloat16

D_MODEL = 1024
HEAD_DIM = 64
H_A = 6
H_B = 4
H_C = 6
MOBA_BLOCK = 256
MOBA_TOPK = 3
T5_BUCKETS = 32
T5_MAX_DIST = 128
RET_ROPE_BASE = 10000.0
Q_LORA = 256
KV_LORA = 128
D_NOPE = 64
D_ROPE = 32
MLA_ROPE_BASE = 10000.0
CONV_W = 3
NORM_EPS = 1e-6
NEG_INF = -1e30
BELOW_NEG_INF = -3.0e38

LANES = 128
A_W = H_A * HEAD_DIM
R_W = H_B * HEAD_DIM
N_IN = 3 * A_W + 4 * R_W + Q_LORA + KV_LORA + D_ROPE
N_IN_PAD = -(-N_IN // LANES) * LANES
M_W = N_IN_PAD - 3 * A_W - 4 * R_W
QCAT_W = 2 * LANES
FFN_CH = 256
VMEM_LIMIT = 56 * 1024 * 1024

_NT = (((1,), (1,)), ((), ()))
_TN = (((0,), (0,)), ((), ()))


def _cparams(sem):
    return pltpu.CompilerParams(dimension_semantics=sem, vmem_limit_bytes=VMEM_LIMIT)


def _rms(x, g):
    return x * lax.rsqrt(jnp.mean(x * x, axis=-1, keepdims=True) + NORM_EPS) * g


def _rope_slab(x, cos, sin_signed, group):
    half = group // 2
    lane = lax.broadcasted_iota(jnp.int32, (1, LANES), 1)
    fwd = pltpu.roll(x, LANES - half, axis=1)
    bwd = pltpu.roll(x, half, axis=1)
    swapped = jnp.where((lane & (group - 1)) < half, fwd, bwd)
    return x * cos + swapped * sin_signed


def _in_proj_kernel(x_ref, g_ref, w_ref, q_ref, k_ref, v_ref, r_ref, m_ref):
    h = _rms(x_ref[...], g_ref[...]).astype(BF16)
    y = jnp.dot(h, w_ref[...], preferred_element_type=F32)
    q_ref[...] = y[:, 0:A_W]
    k_ref[...] = y[:, A_W:2 * A_W]
    v_ref[...] = y[:, 2 * A_W:3 * A_W]
    r_ref[...] = y[:, 3 * A_W:3 * A_W + 4 * R_W]
    m_ref[...] = y[:, 3 * A_W + 4 * R_W:]


def _in_proj(x, g, w):
    m = x.shape[0]
    tm = min(m, 512)
    row = lambda w_: pl.BlockSpec((tm, w_), lambda i: (i, 0))
    full = lambda a: pl.BlockSpec(a.shape, lambda i: (0,) * a.ndim)
    return pl.pallas_call(
        _in_proj_kernel,
        grid=(m // tm,),
        in_specs=[row(D_MODEL), full(g), full(w)],
        out_specs=[row(A_W), row(A_W), row(A_W), row(4 * R_W), row(M_W)],
        out_shape=[jax.ShapeDtypeStruct((m, w_), F32) for w_ in (A_W, A_W, A_W, 4 * R_W, M_W)],
        compiler_params=_cparams(("parallel",)),
        name="in_proj",
    )(x, g, w)


def _mla_prep_kernel(m_ref, gq_ref, gkv_ref, wn_ref, wr_ref, wuk_ref, cos_ref, sin_ref,
                     mla_ref, kcat_ref, qcat_ref):
    scale = (D_NOPE + D_ROPE) ** -0.5
    m = m_ref[...]
    cos = cos_ref[...]
    sin = sin_ref[...]
    lane = lax.broadcasted_iota(jnp.int32, (1, LANES), 1)
    ckv = _rms(m[:, Q_LORA:Q_LORA + KV_LORA], gkv_ref[...])
    kr = _rope_slab(m[:, Q_LORA + KV_LORA:], cos, sin, D_ROPE)
    kr = jnp.where(lane < D_ROPE, kr, 0.0)
    mla_ref[:, 0:KV_LORA] = ckv
    mla_ref[:, KV_LORA:] = kr[:, 0:D_ROPE]
    kcat_ref[:, 0:LANES] = ckv.astype(BF16)
    kcat_ref[:, LANES:] = kr.astype(BF16)
    cq = _rms(m[:, 0:Q_LORA], gq_ref[...]).astype(BF16)
    q_nope = jnp.dot(cq, wn_ref[...], preferred_element_type=F32)
    q_rope = jnp.dot(cq, wr_ref[...], preferred_element_type=F32)
    q_lat = jnp.dot(q_nope.astype(BF16), wuk_ref[...], preferred_element_type=F32)
    per_slab = LANES // D_ROPE
    for h in range(H_C):
        s = h // per_slab
        rot = _rope_slab(q_rope[:, s * LANES:(s + 1) * LANES], cos, sin, D_ROPE)
        sh = (h % per_slab) * D_ROPE
        piece = rot if sh == 0 else pltpu.roll(rot, LANES - sh, axis=1)
        piece = jnp.where(lane < D_ROPE, piece, 0.0) * scale
        qcat_ref[:, h * QCAT_W:h * QCAT_W + LANES] = (q_lat[:, h * LANES:(h + 1) * LANES] * scale).astype(BF16)
        qcat_ref[:, h * QCAT_W + LANES:(h + 1) * QCAT_W] = piece.astype(BF16)


def _mla_prep(m_in, gq, gkv, wn, wr, wuk, cos, sin, rows_per_seq):
    m = m_in.shape[0]
    tm = min(m, 512)
    row = lambda w_: pl.BlockSpec((tm, w_), lambda i: (i, 0))
    full = lambda a: pl.BlockSpec(a.shape, lambda i: (0,) * a.ndim)
    if cos.shape[0] == rows_per_seq and rows_per_seq >= tm:
        nt = rows_per_seq // tm
        tab = pl.BlockSpec((tm, LANES), lambda i: (i % nt, 0))
    else:
        tab = pl.BlockSpec((tm, LANES), lambda i: (0, 0))
    return pl.pallas_call(
        _mla_prep_kernel,
        grid=(m // tm,),
        in_specs=[row(M_W), full(gq), full(gkv), full(wn), full(wr), full(wuk), tab, tab],
        out_specs=[row(KV_LORA + D_ROPE), row(QCAT_W), row(H_C * QCAT_W)],
        out_shape=[jax.ShapeDtypeStruct((m, KV_LORA + D_ROPE), F32),
                   jax.ShapeDtypeStruct((m, QCAT_W), BF16),
                   jax.ShapeDtypeStruct((m, H_C * QCAT_W), BF16)],
        compiler_params=_cparams(("parallel",)),
        name="mla_prep",
    )(m_in, gq, gkv, wn, wr, wuk, cos, sin)


def _mla_attn_kernel(q_ref, k_ref, wuv_ref, o_ref, *, tq):
    qi = pl.program_id(1)
    q = jnp.concatenate([q_ref[0, :, h * QCAT_W:(h + 1) * QCAT_W] for h in range(H_C)], axis=0)
    rows = H_C * tq
    row_pos = lax.broadcasted_iota(jnp.int32, (rows, tq), 0) & (tq - 1)
    col_pos = lax.broadcasted_iota(jnp.int32, (rows, tq), 1)
    causal = col_pos <= row_pos

    def block(j, carry, masked):
        m_i, l_i, acc = carry
        kb = k_ref[0, pl.ds(pl.multiple_of(j * tq, tq), tq), :]
        s = lax.dot_general(q, kb, _NT, preferred_element_type=F32)
        if masked:
            s = jnp.where(causal, s, NEG_INF)
        m_new = jnp.maximum(m_i, jnp.max(s, axis=-1, keepdims=True))
        alpha = jnp.exp(m_i - m_new)
        p = jnp.exp(s - m_new)
        l_new = alpha * l_i + jnp.sum(p, axis=-1, keepdims=True)
        acc_new = alpha * acc + jnp.dot(p.astype(BF16), kb[:, 0:KV_LORA], preferred_element_type=F32)
        return m_new, l_new, acc_new

    init = (jnp.full((rows, 1), NEG_INF, F32), jnp.zeros((rows, 1), F32), jnp.zeros((rows, KV_LORA), F32))
    carry = block(qi, init, True)
    m_i, l_i, acc = lax.fori_loop(0, qi, lambda j, c: block(j, c, False), carry)
    o_lat = (acc / l_i).astype(BF16)
    out = jnp.zeros((tq, A_W), F32)
    for h in range(H_C):
        out = out + jnp.dot(o_lat[h * tq:(h + 1) * tq], wuv_ref[h], preferred_element_type=F32)
    o_ref[0] = out


def _mla_attn(qcat, kcat, wuv_exp):
    b, s, _ = qcat.shape
    tq = min(s, 256)
    return pl.pallas_call(
        functools.partial(_mla_attn_kernel, tq=tq),
        grid=(b, s // tq),
        in_specs=[pl.BlockSpec((1, tq, H_C * QCAT_W), lambda i, j: (i, j, 0)),
                  pl.BlockSpec((1, s, QCAT_W), lambda i, j: (i, 0, 0)),
                  pl.BlockSpec(wuv_exp.shape, lambda i, j: (0, 0, 0))],
        out_specs=pl.BlockSpec((1, tq, A_W), lambda i, j: (i, j, 0)),
        out_shape=jax.ShapeDtypeStruct((b, s, A_W), F32),
        compiler_params=_cparams(("parallel", "arbitrary")),
        name="mla_attn",
    )(qcat, kcat, wuv_exp)


def _moba_attn_kernel(q_ref, k_ref, v_ref, bt_ref, o_ref, km_ref, *, nb):
    blk = MOBA_BLOCK
    qi = pl.program_id(2)
    scale = HEAD_DIM ** -0.5
    lane = lax.broadcasted_iota(jnp.int32, (1, LANES), 1)

    @pl.when(qi == 0)
    def _():
        km_ref[...] = jnp.zeros_like(km_ref)
        for n in range(nb):
            km_ref[n:n + 1, :] = jnp.mean(k_ref[0, n * blk:(n + 1) * blk, :], axis=0, keepdims=True)

    q = q_ref[0]
    km = km_ref[...]
    outs = []
    for hh in range(2):
        hmask = (lane // HEAD_DIM) == hh
        qh = jnp.where(hmask, q, 0.0)
        gate = lax.dot_general(qh, km, _NT, precision=lax.Precision.HIGHEST, preferred_element_type=F32)
        gate = jnp.where(lane < qi, gate, NEG_INF)
        qb = (qh * scale).astype(BF16)

        def block(j, carry, first):
            m_i, l_i, acc = carry
            start = pl.multiple_of(j * blk, blk)
            kb = k_ref[0, pl.ds(start, blk), :].astype(BF16)
            vb = v_ref[0, pl.ds(start, blk), :].astype(BF16)
            s = lax.dot_general(qb, kb, _NT, preferred_element_type=F32)
            s = s + bt_ref[hh, jnp.minimum(qi - j, 2)]
            if not first:
                gj = jnp.sum(jnp.where(lane == j, gate, 0.0), axis=-1, keepdims=True)
                beats = (gate > gj) | ((gate == gj) & (lane < j))
                rank = jnp.sum(beats.astype(F32), axis=-1, keepdims=True)
                s = jnp.where(rank < MOBA_TOPK, s, NEG_INF)
            m_new = jnp.maximum(m_i, jnp.max(s, axis=-1, keepdims=True))
            alpha = jnp.exp(m_i - m_new)
            p = jnp.exp(s - m_new)
            l_new = alpha * l_i + jnp.sum(p, axis=-1, keepdims=True)
            acc_new = alpha * acc + jnp.dot(p.astype(BF16), vb, preferred_element_type=F32)
            return m_new, l_new, acc_new

        init = (jnp.full((blk, 1), NEG_INF, F32), jnp.zeros((blk, 1), F32), jnp.zeros((blk, LANES), F32))
        carry = block(qi, init, True)
        m_i, l_i, acc = lax.fori_loop(0, qi, lambda j, c: block(j, c, False), carry)
        outs.append(acc / l_i)
    o_ref[0] = jnp.where(lane < HEAD_DIM, outs[0], outs[1])


def _moba_attn(q, k, v, btab):
    b, s, _ = q.shape
    blk = MOBA_BLOCK
    nb = s // blk
    hp = A_W // LANES
    return pl.pallas_call(
        functools.partial(_moba_attn_kernel, nb=nb),
        grid=(b, hp, nb),
        in_specs=[pl.BlockSpec((1, blk, LANES), lambda i, h, j: (i, j, h)),
                  pl.BlockSpec((1, s, LANES), lambda i, h, j: (i, 0, h)),
                  pl.BlockSpec((1, s, LANES), lambda i, h, j: (i, 0, h)),
                  pl.BlockSpec((2, 3, blk, blk), lambda i, h, j: (h, 0, 0, 0))],
        out_specs=pl.BlockSpec((1, blk, LANES), lambda i, h, j: (i, j, h)),
        out_shape=jax.ShapeDtypeStruct((b, s, A_W), F32),
        scratch_shapes=[pltpu.VMEM((LANES, LANES), F32)],
        compiler_params=_cparams(("parallel", "parallel", "arbitrary")),
        name="moba_attn",
    )(q, k, v, btab)


def _ret_kernel(r_ref, cos_ref, sin_ref, intra_ref, into_ref, outof_ref, carry_ref, bd_ref, gm_ref,
                gn_ref, o_ref, st_ref, s_scr):
    ci = pl.program_id(1)

    @pl.when(ci == 0)
    def _():
        s_scr[...] = jnp.zeros_like(s_scr)

    r = r_ref[0]
    cos = cos_ref[...]
    sin = sin_ref[...]

    def rope(x):
        return jnp.concatenate(
            [_rope_slab(x[:, s * LANES:(s + 1) * LANES], cos[:, s * LANES:(s + 1) * LANES],
                        sin[:, s * LANES:(s + 1) * LANES], HEAD_DIM) for s in range(R_W // LANES)], axis=1)

    q = rope(r[:, 0:R_W])
    k = rope(r[:, R_W:2 * R_W]) * (HEAD_DIM ** -0.5)
    v = r[:, 2 * R_W:3 * R_W]
    gate = r[:, 3 * R_W:]
    lane_head = lax.broadcasted_iota(jnp.int32, (1, R_W), 1) // HEAD_DIM
    kb = k.astype(BF16)
    state = s_scr[...]
    o = jnp.dot((q * into_ref[...]).astype(BF16), state.astype(BF16), preferred_element_type=F32)
    for h in range(H_B):
        qh = jnp.where(lane_head == h, q, 0.0).astype(BF16)
        a = lax.dot_general(qh, kb, _NT, preferred_element_type=F32) * intra_ref[h]
        vh = jnp.where(lane_head == h, v, 0.0).astype(BF16)
        o = o + jnp.dot(a.astype(BF16), vh, preferred_element_type=F32)
    kv = lax.dot_general((k * outof_ref[...]).astype(BF16), v.astype(BF16), _TN, preferred_element_type=F32)
    new_state = state * carry_ref[...] + kv * bd_ref[...]
    s_scr[...] = new_state
    st_ref[0] = new_state
    ms = jnp.dot(o * o, gm_ref[...], precision=lax.Precision.HIGHEST, preferred_element_type=F32)
    ob = o * lax.rsqrt(ms + NORM_EPS) * gn_ref[...]
    o_ref[0] = ob * (gate * jax.nn.sigmoid(gate))


def _retention_tables(c):
    log_g = jnp.log(1.0 - 2.0 ** (-5.0 - jnp.arange(H_B, dtype=F32)))
    i = jnp.arange(c, dtype=F32)
    diff = i[:, None] - i[None, :]
    intra = jnp.where(diff >= 0, jnp.exp(log_g[:, None, None] * jnp.maximum(diff, 0.0)), 0.0)
    lg_lane = jnp.repeat(log_g, HEAD_DIM)
    into = jnp.exp(lg_lane[None, :] * (i[:, None] + 1.0))
    outof = jnp.exp(lg_lane[None, :] * (c - 1.0 - i[:, None]))
    carry = jnp.exp(lg_lane * c)[:, None]
    head = jnp.arange(R_W) // HEAD_DIM
    bd = (head[:, None] == head[None, :]).astype(F32)
    return intra, into, outof, carry, bd, bd / HEAD_DIM


def _retention(r, cos, sin, gn):
    b, s, _ = r.shape
    c = min(s, 256)
    intra, into, outof, carry, bd, gm = _retention_tables(c)
    full = lambda a: pl.BlockSpec(a.shape, lambda i, j: (0,) * a.ndim)
    tab = pl.BlockSpec((c, R_W), lambda i, j: (j, 0))
    return pl.pallas_call(
        _ret_kernel,
        grid=(b, s // c),
        in_specs=[pl.BlockSpec((1, c, 4 * R_W), lambda i, j: (i, j, 0)), tab, tab,
                  full(intra), full(into), full(outof), full(carry), full(bd), full(gm), full(gn)],
        out_specs=[pl.BlockSpec((1, c, R_W), lambda i, j: (i, j, 0)),
                   pl.BlockSpec((1, R_W, R_W), lambda i, j: (i, 0, 0))],
        out_shape=[jax.ShapeDtypeStruct((b, s, R_W), F32), jax.ShapeDtypeStruct((b, R_W, R_W), F32)],
        scratch_shapes=[pltpu.VMEM((R_W, R_W), F32)],
        compiler_params=_cparams(("parallel", "arbitrary")),
        name="retention",
    )(r, cos, sin, intra, into, outof, carry, bd, gm, gn)


def _ret_step_kernel(q_ref, k_ref, v_ref, g_ref, st_ref, cos_ref, sin_ref, dec_ref, gn_ref, o_ref, ns_ref):
    half = HEAD_DIM // 2
    cos = cos_ref[...]
    sin = sin_ref[...]

    def rope_col(x):
        swapped = jnp.concatenate([x[:, half:, :], x[:, :half, :]], axis=1)
        return x * cos + swapped * sin

    q = rope_col(q_ref[0])
    k = rope_col(k_ref[0]) * (HEAD_DIM ** -0.5)
    v = v_ref[0]
    new_state = st_ref[0] * dec_ref[...] + k * v
    ns_ref[0] = new_state
    o = jnp.sum(q * new_state, axis=1, keepdims=True)
    ob = o * lax.rsqrt(jnp.mean(o * o, axis=-1, keepdims=True) + NORM_EPS) * gn_ref[...]
    gate = g_ref[0]
    o_ref[0] = ob * (gate * jax.nn.sigmoid(gate))


def _retention_step(r, state, pos, gn):
    b = r.shape[0]
    col = lambda x: x.reshape(b, H_B, HEAD_DIM, 1)
    rowv = lambda x: x.reshape(b, H_B, 1, HEAD_DIM)
    half = HEAD_DIM // 2
    inv = RET_ROPE_BASE ** (-jnp.arange(half, dtype=F32) / half)
    ang = jnp.asarray(pos, F32) * inv
    cos = jnp.concatenate([jnp.cos(ang), jnp.cos(ang)]).reshape(1, HEAD_DIM, 1)
    sin = jnp.concatenate([-jnp.sin(ang), jnp.sin(ang)]).reshape(1, HEAD_DIM, 1)
    log_g = jnp.log(1.0 - 2.0 ** (-5.0 - jnp.arange(H_B, dtype=F32)))
    dec = jnp.exp(log_g * 1.0).reshape(H_B, 1, 1)
    gn4 = gn.reshape(H_B, 1, HEAD_DIM)
    cspec = pl.BlockSpec((1, H_B, HEAD_DIM, 1), lambda i: (i, 0, 0, 0))
    rspec = pl.BlockSpec((1, H_B, 1, HEAD_DIM), lambda i: (i, 0, 0, 0))
    sspec = pl.BlockSpec((1, H_B, HEAD_DIM, HEAD_DIM), lambda i: (i, 0, 0, 0))
    full = lambda a: pl.BlockSpec(a.shape, lambda i: (0,) * a.ndim)
    o, ns = pl.pallas_call(
        _ret_step_kernel,
        grid=(b,),
        in_specs=[cspec, cspec, rspec, rspec, sspec, full(cos), full(sin), full(dec), full(gn4)],
        out_specs=[rspec, sspec],
        out_shape=[jax.ShapeDtypeStruct((b, H_B, 1, HEAD_DIM), F32),
                   jax.ShapeDtypeStruct((b, H_B, HEAD_DIM, HEAD_DIM), F32)],
        compiler_params=_cparams(("parallel",)),
        name="retention_step",
    )(col(r[:, 0:R_W]), col(r[:, R_W:2 * R_W]), rowv(r[:, 2 * R_W:3 * R_W]), rowv(r[:, 3 * R_W:]),
      state, cos, sin, dec, gn4)
    return o.reshape(b, R_W), ns


def _out_proj_kernel(x_ref, a_ref, b_ref, c_ref, wa_ref, wb_ref, wc_ref, y_ref):
    y = x_ref[...]
    y = y + jnp.dot(a_ref[...].astype(BF16), wa_ref[...], preferred_element_type=F32)
    y = y + jnp.dot(b_ref[...].astype(BF16), wb_ref[...], preferred_element_type=F32)
    y = y + jnp.dot(c_ref[...].astype(BF16), wc_ref[...], preferred_element_type=F32)
    y_ref[...] = y


def _out_proj(x, oa, ob, oc, wa, wb, wc):
    m = x.shape[0]
    tm = min(m, 512)
    row = lambda w_: pl.BlockSpec((tm, w_), lambda i: (i, 0))
    full = lambda a: pl.BlockSpec(a.shape, lambda i: (0,) * a.ndim)
    return pl.pallas_call(
        _out_proj_kernel,
        grid=(m // tm,),
        in_specs=[row(D_MODEL), row(A_W), row(R_W), row(A_W), full(wa), full(wb), full(wc)],
        out_specs=row(D_MODEL),
        out_shape=jax.ShapeDtypeStruct((m, D_MODEL), F32),
        compiler_params=_cparams(("parallel",)),
        name="out_proj",
    )(x, oa, ob, oc, wa, wb, wc)


_HALO = 16


def _ffn_kernel(xm_ref, xh_ref, st_ref, g2_ref, wup_ref, cw_ref, cb_ref, wdn_ref, fg_ref,
                y_ref, cn_ref, hs_ref, ug_ref, uv_ref, *, tm, nch, final):
    i = pl.program_id(1)
    g2 = g2_ref[...]
    hs_ref[0:_HALO, :] = _rms(xh_ref[0], g2).astype(BF16)
    hs_ref[_HALO:, :] = _rms(xm_ref[0], g2).astype(BF16)
    y_ref[0] = xm_ref[0]
    lo = _HALO - (CONV_W - 1)

    def chunk(c, carry):
        convs = []
        for part, u_ref in ((0, ug_ref), (1, uv_ref)):
            u_ref[...] = jnp.dot(hs_ref[...], wup_ref[part, c], preferred_element_type=F32)

            @pl.when(i == 0)
            def _():
                u_ref[lo:_HALO, :] = st_ref[0, part, c]

            w = cw_ref[part, c]
            conv = cb_ref[part, c]
            for t in range(CONV_W):
                conv = conv + w[t:t + 1, :] * u_ref[lo + t:lo + t + tm, :]
            convs.append(conv)
            cn_ref[0, part, c] = u_ref[_HALO + tm - (CONV_W - 1):_HALO + tm, :]
        act = (convs[0] * jax.nn.sigmoid(convs[0])) * convs[1]
        y_ref[0] += jnp.dot(act.astype(BF16), wdn_ref[c], preferred_element_type=F32)
        return carry

    lax.fori_loop(0, nch, chunk, 0)
    if final:
        y_ref[0] = _rms(y_ref[0], fg_ref[...])


def _ffn(x, state5, g2, wup, cw, cb, wdn, fg, final):
    b, s, _ = x.shape
    nch = wdn.shape[0]
    tm = min(s, 512)
    nt = s // tm
    hb = tm // _HALO
    full = lambda a: pl.BlockSpec(a.shape, lambda i, j: (0,) * a.ndim)
    st_spec = pl.BlockSpec((1,) + state5.shape[1:], lambda i, j: (i, 0, 0, 0, 0))
    return pl.pallas_call(
        functools.partial(_ffn_kernel, tm=tm, nch=nch, final=final),
        grid=(b, nt),
        in_specs=[pl.BlockSpec((1, tm, D_MODEL), lambda i, j: (i, j, 0)),
                  pl.BlockSpec((1, _HALO, D_MODEL), lambda i, j: (i, jnp.maximum(j * hb - 1, 0), 0)),
                  st_spec, full(g2), full(wup), full(cw), full(cb), full(wdn), full(fg)],
        out_specs=[pl.BlockSpec((1, tm, D_MODEL), lambda i, j: (i, j, 0)), st_spec],
        out_shape=[jax.ShapeDtypeStruct((b, s, D_MODEL), F32), jax.ShapeDtypeStruct(state5.shape, F32)],
        scratch_shapes=[pltpu.VMEM((tm + _HALO, D_MODEL), BF16),
                        pltpu.VMEM((tm + _HALO, FFN_CH), F32),
                        pltpu.VMEM((tm + _HALO, FFN_CH), F32)],
        compiler_params=_cparams(("parallel", "arbitrary")),
        name="conv_ffn",
    )(x, x, state5, g2, wup, cw, cb, wdn, fg)


def _ffn_step_kernel(x_ref, st_ref, g2_ref, wup_ref, cw_ref, cb_ref, wdn_ref, fg_ref,
                     y_ref, cn_ref, hs_ref, *, final):
    c = pl.program_id(0)

    @pl.when(c == 0)
    def _():
        hs_ref[...] = _rms(x_ref[...], g2_ref[...]).astype(BF16)
        y_ref[...] = x_ref[...]

    convs = []
    for part in range(2):
        u = jnp.dot(hs_ref[...], wup_ref[part, 0], preferred_element_type=F32)
        w = cw_ref[part, 0]
        s0 = st_ref[0, part, 0]
        s1 = st_ref[0, part, 1]
        convs.append(cb_ref[part, 0] + w[0:1, :] * s0 + w[1:2, :] * s1 + w[2:3, :] * u)
        cn_ref[0, part, 0] = s1
        cn_ref[0, part, 1] = u
    act = (convs[0] * jax.nn.sigmoid(convs[0])) * convs[1]
    y_ref[...] += jnp.dot(act.astype(BF16), wdn_ref[0], preferred_element_type=F32)
    if final:
        @pl.when(c == pl.num_programs(0) - 1)
        def _():
            y_ref[...] = _rms(y_ref[...], fg_ref[...])


def _ffn_step(x, state5, g2, wup, cw, cb, wdn, fg, final):
    b = x.shape[0]
    nch = wdn.shape[0]
    full = lambda a: pl.BlockSpec(a.shape, lambda c: (0,) * a.ndim)
    st_spec = pl.BlockSpec((1,) + state5.shape[1:], lambda c: (c, 0, 0, 0, 0))
    return pl.pallas_call(
        functools.partial(_ffn_step_kernel, final=final),
        grid=(nch,),
        in_specs=[full(x), st_spec, full(g2),
                  pl.BlockSpec((2, 1, D_MODEL, FFN_CH), lambda c: (0, c, 0, 0)),
                  pl.BlockSpec((2, 1, CONV_W, FFN_CH), lambda c: (0, c, 0, 0)),
                  pl.BlockSpec((2, 1, 1, FFN_CH), lambda c: (0, c, 0, 0)),
                  pl.BlockSpec((1, FFN_CH, D_MODEL), lambda c: (c, 0, 0)),
                  full(fg)],
        out_specs=[full(x), st_spec],
        out_shape=[jax.ShapeDtypeStruct((b, D_MODEL), F32), jax.ShapeDtypeStruct(state5.shape, F32)],
        scratch_shapes=[pltpu.VMEM((b, D_MODEL), BF16)],
        compiler_params=_cparams(("arbitrary",)),
        name="conv_ffn_step",
    )(x, state5, g2, wup, cw, cb, wdn, fg)


def _moba_sel_kernel(pt_ref, q_ref, gm_ref, kc_ref, sel_ref, buf, sem, km_ref, *, pc, nbp):
    b = pl.program_id(0)
    nseq = pl.num_programs(0)
    npages = pt_ref.shape[1]
    nchunks = npages // pc
    ppb = MOBA_BLOCK // buf.shape[2]
    bpc = pc // ppb

    def copies(bb, c, slot):
        return [pltpu.make_async_copy(kc_ref.at[pt_ref[bb, c * pc + p]], buf.at[slot, p], sem.at[slot])
                for p in range(pc)]

    @pl.when(b == 0)
    def _():
        km_ref[...] = jnp.zeros_like(km_ref)
        for cp in copies(0, 0, 0):
            cp.start()

    def chunk(c, carry):
        slot = (b * nchunks + c) & 1

        @pl.when(c + 1 < nchunks)
        def _():
            for cp in copies(b, c + 1, 1 - slot):
                cp.start()

        @pl.when((c + 1 == nchunks) & (b + 1 < nseq))
        def _():
            for cp in copies(b + 1, 0, 1 - slot):
                cp.start()

        for cp in copies(b, c, slot):
            cp.wait()
        for n in range(bpc):
            rows = buf[slot, n * ppb:(n + 1) * ppb].reshape(MOBA_BLOCK, A_W)
            km_ref[pl.ds(c * bpc + n, 1), :] = jnp.sum(rows, axis=0, keepdims=True) * (1.0 / MOBA_BLOCK)
        return carry

    lax.fori_loop(0, nchunks, chunk, 0)
    prod = km_ref[...] * q_ref[0]
    gate = jnp.dot(prod, gm_ref[...], precision=lax.Precision.HIGHEST, preferred_element_type=F32)
    blk_id = lax.broadcasted_iota(jnp.int32, gate.shape, 0).astype(F32)
    gate = jnp.where(blk_id < nbp, gate, BELOW_NEG_INF)
    row_id = lax.broadcasted_iota(jnp.int32, (8, LANES), 0)
    out = jnp.zeros((8, LANES), F32)
    for t in range(MOBA_TOPK):
        mx = jnp.max(gate, axis=0, keepdims=True)
        idx = jnp.min(jnp.where(gate == mx, blk_id, 1e9), axis=0, keepdims=True)
        out = jnp.where(row_id == t, idx, out)
        gate = jnp.where(blk_id == idx, BELOW_NEG_INF, gate)
    sel_ref[0] = out.astype(jnp.int32)


def _moba_select(page_table, q, kcache, nbp):
    b, npages = page_table.shape
    page = kcache.shape[1]
    pc = min(npages, 8)
    nbp_pad = -(-nbp // 8) * 8
    head = jnp.arange(A_W) // HEAD_DIM
    gmat = (head[:, None] == jnp.arange(LANES)[None, :]).astype(F32)
    grid_spec = pltpu.PrefetchScalarGridSpec(
        num_scalar_prefetch=1,
        grid=(b,),
        in_specs=[pl.BlockSpec((1, 1, A_W), lambda i, pt: (i, 0, 0)),
                  pl.BlockSpec(gmat.shape, lambda i, pt: (0, 0)),
                  pl.BlockSpec(memory_space=pl.ANY)],
        out_specs=pl.BlockSpec((1, 8, LANES), lambda i, pt: (i, 0, 0)),
        scratch_shapes=[pltpu.VMEM((2, pc, page, A_W), F32),
                        pltpu.SemaphoreType.DMA((2,)),
                        pltpu.VMEM((nbp_pad, A_W), F32)],
    )
    return pl.pallas_call(
        functools.partial(_moba_sel_kernel, pc=pc, nbp=nbp),
        grid_spec=grid_spec,
        out_shape=jax.ShapeDtypeStruct((b, 8, LANES), jnp.int32),
        compiler_params=_cparams(("arbitrary",)),
        name="moba_select",
    )(page_table, q.reshape(b, 1, A_W), gmat, kcache)


def _moba_dec_kernel(pt_ref, sel_ref, q_ref, kn_ref, vn_ref, bl_ref, bs_ref, kc_ref, vc_ref, o_ref,
                     kbuf, vbuf, sem, *, nbp, ppb):
    b = pl.program_id(0)
    nseq = pl.num_programs(0)
    scale = HEAD_DIM ** -0.5
    page = kbuf.shape[3]
    per_slab = LANES // HEAD_DIM
    lane = lax.broadcasted_iota(jnp.int32, (1, LANES), 1)

    def copies(bb, slot):
        out = []
        for h in range(H_A):
            cols = pl.ds((h // per_slab) * LANES, LANES)
            for t in range(MOBA_TOPK):
                n = sel_ref[bb, t * H_A + h]
                for p in range(ppb):
                    pg = pt_ref[bb, n * ppb + p]
                    out.append(pltpu.make_async_copy(kc_ref.at[pg, :, cols], kbuf.at[slot, h, t * ppb + p],
                                                     sem.at[0, slot]))
                    out.append(pltpu.make_async_copy(vc_ref.at[pg, :, cols], vbuf.at[slot, h, t * ppb + p],
                                                     sem.at[1, slot]))
        return out

    @pl.when(b == 0)
    def _():
        for cp in copies(0, 0):
            cp.start()

    slot = b & 1

    @pl.when(b + 1 < nseq)
    def _():
        for cp in copies(b + 1, 1 - slot):
            cp.start()

    for cp in copies(b, slot):
        cp.wait()

    nkeys = MOBA_TOPK * ppb * page
    outs = []
    for h in range(H_A):
        s_idx = h // per_slab
        hmask = (lane // HEAD_DIM) == (h % per_slab)
        qh = jnp.where(hmask, q_ref[0, s_idx], 0.0) * scale
        kh = kbuf[slot, h].reshape(nkeys, LANES)
        vh = vbuf[slot, h].reshape(nkeys, LANES)
        s_sel = jnp.sum(kh * qh, axis=-1, keepdims=True)
        far = bs_ref[h, 0]
        bias = jnp.concatenate(
            [jnp.where(sel_ref[b, t * H_A + h] == nbp - 1, bl_ref[h], far) for t in range(MOBA_TOPK)], axis=0)
        s_sel = s_sel + bias
        s_new = jnp.sum(kn_ref[0, s_idx] * qh, axis=-1, keepdims=True) + bs_ref[h, 1]
        mx = jnp.maximum(jnp.max(s_sel, axis=0, keepdims=True), s_new)
        p_sel = jnp.exp(s_sel - mx)
        p_new = jnp.exp(s_new - mx)
        denom = jnp.sum(p_sel, axis=0, keepdims=True) + p_new
        o = jnp.sum(p_sel * vh, axis=0, keepdims=True) + p_new * vn_ref[0, s_idx]
        outs.append(jnp.where(hmask, o / denom, 0.0))
    for s_idx in range(H_A // per_slab):
        o_ref[0, s_idx] = sum(outs[s_idx * per_slab:(s_idx + 1) * per_slab])


def _t5_bucket(dist):
    n = jnp.maximum(dist, 0)
    max_exact = T5_BUCKETS // 2
    nf = jnp.maximum(n, 1).astype(F32)
    large = max_exact + (jnp.log(nf / max_exact) / math.log(T5_MAX_DIST / max_exact)
                         * (T5_BUCKETS - max_exact)).astype(jnp.int32)
    large = jnp.minimum(large, T5_BUCKETS - 1)
    return jnp.where(n < max_exact, n, large)


def _moba_decode(page_table, sel, q, k_new, v_new, kcache, vcache, rel_bias, past_len):
    b, npages = page_table.shape
    page = kcache.shape[1]
    ppb = MOBA_BLOCK // page
    nbp = past_len // MOBA_BLOCK
    bias_t = rel_bias.astype(F32).T
    off = jnp.arange(MOBA_BLOCK)
    dist_last = past_len - ((nbp - 1) * MOBA_BLOCK + off)
    b_last = bias_t[:, _t5_bucket(dist_last)][:, :, None]
    far_bucket = _t5_bucket(jnp.asarray(past_len - (nbp - 1) * MOBA_BLOCK + 1))
    b_sc = jnp.stack([bias_t[:, far_bucket], bias_t[:, _t5_bucket(jnp.asarray(0))]], axis=1)
    sel_flat = sel[:, :MOBA_TOPK, :H_A].reshape(b, MOBA_TOPK * H_A)
    nslab = A_W // LANES
    hv = lambda x: x.reshape(b, nslab, 1, LANES)
    hspec = pl.BlockSpec((1, nslab, 1, LANES), lambda i, pt, sl: (i, 0, 0, 0))
    grid_spec = pltpu.PrefetchScalarGridSpec(
        num_scalar_prefetch=2,
        grid=(b,),
        in_specs=[hspec, hspec, hspec,
                  pl.BlockSpec(b_last.shape, lambda i, pt, sl: (0, 0, 0)),
                  pl.BlockSpec(memory_space=pltpu.SMEM),
                  pl.BlockSpec(memory_space=pl.ANY),
                  pl.BlockSpec(memory_space=pl.ANY)],
        out_specs=hspec,
        scratch_shapes=[pltpu.VMEM((2, H_A, MOBA_TOPK * ppb, page, LANES), F32),
                        pltpu.VMEM((2, H_A, MOBA_TOPK * ppb, page, LANES), F32),
                        pltpu.SemaphoreType.DMA((2, 2))],
    )
    o = pl.pallas_call(
        functools.partial(_moba_dec_kernel, nbp=nbp, ppb=ppb),
        grid_spec=grid_spec,
        out_shape=jax.ShapeDtypeStruct((b, nslab, 1, LANES), F32),
        compiler_params=_cparams(("arbitrary",)),
        name="moba_decode",
    )(page_table, sel_flat, hv(q), hv(k_new), hv(v_new), b_last, b_sc, kcache, vcache)
    return o.reshape(b, A_W)


def _mla_dec_kernel(pt_ref, q_ref, kn_ref, mc_ref, o_ref, buf, sem, *, pc):
    b = pl.program_id(0)
    nseq = pl.num_programs(0)
    npages = pt_ref.shape[1]
    nchunks = npages // pc
    page = buf.shape[2]
    nkeys = pc * page
    q = q_ref[0]
    q_lat = q[:, 0:KV_LORA]
    q_rope = q[:, KV_LORA:KV_LORA + D_ROPE]

    def copies(bb, c, slot):
        return [pltpu.make_async_copy(mc_ref.at[pt_ref[bb, c * pc + p]], buf.at[slot, p], sem.at[slot])
                for p in range(pc)]

    @pl.when(b == 0)
    def _():
        for cp in copies(0, 0, 0):
            cp.start()

    def chunk(c, carry):
        m_i, l_i, acc = carry
        slot = (b * nchunks + c) & 1

        @pl.when(c + 1 < nchunks)
        def _():
            for cp in copies(b, c + 1, 1 - slot):
                cp.start()

        @pl.when((c + 1 == nchunks) & (b + 1 < nseq))
        def _():
            for cp in copies(b + 1, 0, 1 - slot):
                cp.start()

        for cp in copies(b, c, slot):
            cp.wait()
        kv = buf[slot].reshape(nkeys, KV_LORA + D_ROPE).astype(BF16)
        lat = kv[:, 0:KV_LORA]
        s = (lax.dot_general(q_lat, lat, _NT, preferred_element_type=F32)
             + lax.dot_general(q_rope, kv[:, KV_LORA:], _NT, preferred_element_type=F32))
        m_new = jnp.maximum(m_i, jnp.max(s, axis=-1, keepdims=True))
        alpha = jnp.exp(m_i - m_new)
        p = jnp.exp(s - m_new)
        l_new = alpha * l_i + jnp.sum(p, axis=-1, keepdims=True)
        acc_new = alpha * acc + jnp.dot(p.astype(BF16), lat, preferred_element_type=F32)
        return m_new, l_new, acc_new

    init = (jnp.full((8, 1), NEG_INF, F32), jnp.zeros((8, 1), F32), jnp.zeros((8, KV_LORA), F32))
    m_i, l_i, acc = lax.fori_loop(0, nchunks, chunk, init)
    kn = kn_ref[0].astype(F32)
    s_new = jnp.sum(q.astype(F32) * kn, axis=-1, keepdims=True)
    m_new = jnp.maximum(m_i, s_new)
    alpha = jnp.exp(m_i - m_new)
    p_new = jnp.exp(s_new - m_new)
    l_fin = alpha * l_i + p_new
    acc_fin = alpha * acc + p_new * kn[:, 0:KV_LORA]
    o_ref[0] = acc_fin / l_fin


def _mla_decode(page_table, qcat, kcat_new, mcache):
    b, npages = page_table.shape
    page = mcache.shape[1]
    pc = min(npages, 16)
    q8 = jnp.pad(qcat.reshape(b, H_C, QCAT_W), ((0, 0), (0, 8 - H_C), (0, 0)))
    grid_spec = pltpu.PrefetchScalarGridSpec(
        num_scalar_prefetch=1,
        grid=(b,),
        in_specs=[pl.BlockSpec((1, 8, QCAT_W), lambda i, pt: (i, 0, 0)),
                  pl.BlockSpec((1, 1, QCAT_W), lambda i, pt: (i, 0, 0)),
                  pl.BlockSpec(memory_space=pl.ANY)],
        out_specs=pl.BlockSpec((1, 8, KV_LORA), lambda i, pt: (i, 0, 0)),
        scratch_shapes=[pltpu.VMEM((2, pc, page, KV_LORA + D_ROPE), F32),
                        pltpu.SemaphoreType.DMA((2,))],
    )
    o_lat = pl.pallas_call(
        functools.partial(_mla_dec_kernel, pc=pc),
        grid_spec=grid_spec,
        out_shape=jax.ShapeDtypeStruct((b, 8, KV_LORA), F32),
        compiler_params=_cparams(("arbitrary",)),
        name="mla_decode",
    )(page_table, q8, kcat_new.reshape(b, 1, QCAT_W), mcache)
    return o_lat[:, :H_C, :].reshape(b, H_C * KV_LORA)


def _matmul_kernel(a_ref, w_ref, o_ref):
    o_ref[...] = jnp.dot(a_ref[...].astype(BF16), w_ref[...], preferred_element_type=F32)


def _matmul(a, w):
    full = lambda x: pl.BlockSpec(x.shape, lambda: (0,) * x.ndim)
    return pl.pallas_call(
        _matmul_kernel,
        in_specs=[full(a), full(w)],
        out_specs=pl.BlockSpec((a.shape[0], w.shape[1]), lambda: (0, 0)),
        out_shape=jax.ShapeDtypeStruct((a.shape[0], w.shape[1]), F32),
        name="latent_out",
    )(a, w)


def _rope_tables(pos, base, group, width):
    half = group // 2
    inv = base ** (-jnp.arange(half, dtype=F32) / half)
    ang = pos.astype(F32)[:, None] * inv[None, :]
    cos = jnp.concatenate([jnp.cos(ang), jnp.cos(ang)], axis=1)
    sin = jnp.concatenate([-jnp.sin(ang), jnp.sin(ang)], axis=1)
    reps = width // group
    return jnp.tile(cos, (1, reps)), jnp.tile(sin, (1, reps))


def _moba_bias_tiles(rel_bias):
    bias_t = rel_bias.astype(F32).T
    r = jnp.arange(MOBA_BLOCK)
    d0 = r[:, None] - r[None, :]
    own = jnp.where((d0 >= 0)[None], bias_t[:, _t5_bucket(d0)], NEG_INF)
    prev = bias_t[:, _t5_bucket(d0 + MOBA_BLOCK)]
    far = jnp.broadcast_to(bias_t[:, _t5_bucket(jnp.asarray(MOBA_BLOCK + 1))][:, None, None], own.shape)
    return jnp.stack([own, prev, far], axis=1)


def _layer_weights(w_in, w_uq, w_uk, w_uv, w_o, w_up, conv_w, conv_b, w_down):
    d_ff = w_down.shape[0]
    nch = d_ff // FFN_CH
    w = {}
    w["in"] = jnp.pad(w_in, ((0, 0), (0, N_IN_PAD - N_IN))).astype(BF16)
    uq = w_uq.reshape(Q_LORA, H_C, D_NOPE + D_ROPE)
    w["uq_nope"] = uq[:, :, :D_NOPE].reshape(Q_LORA, H_C * D_NOPE).astype(BF16)
    w["uq_rope"] = jnp.pad(uq[:, :, D_NOPE:].reshape(Q_LORA, H_C * D_ROPE),
                           ((0, 0), (0, 2 * LANES - H_C * D_ROPE))).astype(BF16)
    eye = jnp.eye(H_C, dtype=F32)
    w["uk_bd"] = jnp.einsum("rhd,hg->hdgr", w_uk, eye).reshape(H_C * D_NOPE, H_C * KV_LORA).astype(BF16)
    w["uv_exp"] = jnp.einsum("rhd,hg->hrgd", w_uv, eye).reshape(H_C, KV_LORA, H_C * HEAD_DIM).astype(BF16)
    w["o_a"] = w_o[0:A_W].astype(BF16)
    w["o_b"] = w_o[A_W:A_W + R_W].astype(BF16)
    w["o_c"] = w_o[A_W + R_W:].astype(BF16)
    w["up"] = w_up.reshape(D_MODEL, 2, nch, FFN_CH).transpose(1, 2, 0, 3).astype(BF16)
    w["conv_w"] = conv_w.reshape(CONV_W, 2, nch, FFN_CH).transpose(1, 2, 0, 3)
    w["conv_b"] = conv_b.reshape(2, nch, 1, FFN_CH)
    w["down"] = w_down.reshape(nch, FFN_CH, D_MODEL).astype(BF16)
    return w


def _state_to_chunks(state, nch):
    b = state.shape[0]
    return state.reshape(b, CONV_W - 1, 2, nch, FFN_CH).transpose(0, 2, 3, 1, 4)


def _chunks_to_state(c5):
    b, _, nch, rows, ch = c5.shape
    return c5.transpose(0, 3, 1, 2, 4).reshape(b, rows, 2 * nch * ch)


def _ret_state_blocks(sfull):
    b = sfull.shape[0]
    s6 = sfull.reshape(b, H_B, HEAD_DIM, H_B, HEAD_DIM)
    return jnp.stack([s6[:, h, :, h, :] for h in range(H_B)], axis=1)


def _prompt_layer(x, w, g1, gq, gkv, gret, g2, fg, btab, tabs, final):
    b, s, _ = x.shape
    m = b * s
    x2 = x.reshape(m, D_MODEL)
    a_q, a_k, a_v, r, mm = _in_proj(x2, g1, w["in"])
    mla_new, kcat, qcat = _mla_prep(mm, gq, gkv, w["uq_nope"], w["uq_rope"], w["uk_bd"],
                                    tabs["mla_cos"], tabs["mla_sin"], s)
    o_a = _moba_attn(a_q.reshape(b, s, A_W), a_k.reshape(b, s, A_W), a_v.reshape(b, s, A_W), btab)
    o_b, sfull = _retention(r.reshape(b, s, 4 * R_W), tabs["ret_cos"], tabs["ret_sin"], gret)
    o_c = _mla_attn(qcat.reshape(b, s, H_C * QCAT_W), kcat.reshape(b, s, QCAT_W), w["uv_exp"])
    x1 = _out_proj(x2, o_a.reshape(m, A_W), o_b.reshape(m, R_W), o_c.reshape(m, A_W), w["o_a"], w["o_b"], w["o_c"])
    nch = w["down"].shape[0]
    st0 = jnp.zeros((b, 2, nch, CONV_W - 1, FFN_CH), F32)
    y, cn = _ffn(x1.reshape(b, s, D_MODEL), st0, g2, w["up"], w["conv_w"], w["conv_b"], w["down"], fg, final)
    return (y, a_k.reshape(b, s, H_A, HEAD_DIM), a_v.reshape(b, s, H_A, HEAD_DIM),
            mla_new.reshape(b, s, KV_LORA + D_ROPE), _ret_state_blocks(sfull), _chunks_to_state(cn))


def _sample_layer(x, w, g1, gq, gkv, gret, g2, fg, rel_bias, tabs, final,
                  kcache, vcache, mcache, ret_state, conv_state, page_table, past_len):
    b = x.shape[0]
    a_q, a_k, a_v, r, mm = _in_proj(x, g1, w["in"])
    mla_new, kcat, qcat = _mla_prep(mm, gq, gkv, w["uq_nope"], w["uq_rope"], w["uk_bd"],
                                    tabs["mla_cos_s"], tabs["mla_sin_s"], 1)
    nbp = past_len // MOBA_BLOCK
    sel = _moba_select(page_table, a_q, kcache, nbp)
    o_a = _moba_decode(page_table, sel, a_q, a_k, a_v, kcache, vcache, rel_bias, past_len)
    o_b, ret_new = _retention_step(r, ret_state, past_len, gret)
    o_lat = _mla_decode(page_table, qcat, kcat, mcache)
    o_c = _matmul(o_lat, w["uv_exp"].reshape(H_C * KV_LORA, A_W))
    x1 = _out_proj(x, o_a, o_b, o_c, w["o_a"], w["o_b"], w["o_c"])
    nch = w["down"].shape[0]
    st = _state_to_chunks(conv_state, nch).transpose(2, 1, 3, 0, 4)
    y, cn = _ffn_step(x1, st, g2, w["up"], w["conv_w"], w["conv_b"], w["down"], fg, final)
    conv_new = _chunks_to_state(cn.transpose(3, 1, 0, 2, 4))
    return (y, a_k.reshape(b, 1, H_A, HEAD_DIM), a_v.reshape(b, 1, H_A, HEAD_DIM),
            mla_new.reshape(b, 1, KV_LORA + D_ROPE), ret_new, conv_new)


def kernel(x_prompt, x_sample, cache_moba_k, cache_moba_v, cache_mla, state_ret, state_conv, page_table, norm1_g, w_in, rel_bias, ret_norm_g, mla_q_norm_g, mla_kv_norm_g, w_uq, w_uk, w_uv, w_o, norm2_g, w_up, conv_w, conv_b, w_down, final_norm_g):
    depth = w_in.shape[0]
    bp, s, _ = x_prompt.shape
    bs, s_dec, _ = x_sample.shape
    n_pool, page = cache_moba_k.shape[1], cache_moba_k.shape[2]
    past_len = page_table.shape[1] * page
    assert s % MOBA_BLOCK == 0 and s_dec == 1 and MOBA_BLOCK % page == 0
    assert past_len % MOBA_BLOCK == 0 and past_len // MOBA_BLOCK >= MOBA_TOPK
    assert w_down.shape[1] % FFN_CH == 0

    pos_p = jnp.arange(s)
    pos_s = jnp.full((bs,), past_len)
    tabs = {}
    tabs["ret_cos"], tabs["ret_sin"] = _rope_tables(pos_p, RET_ROPE_BASE, HEAD_DIM, R_W)
    tabs["mla_cos"], tabs["mla_sin"] = _rope_tables(pos_p, MLA_ROPE_BASE, D_ROPE, LANES)
    tabs["mla_cos_s"], tabs["mla_sin_s"] = _rope_tables(pos_s, MLA_ROPE_BASE, D_ROPE, LANES)
    btab = _moba_bias_tiles(rel_bias)
    fg = final_norm_g.reshape(1, D_MODEL)

    xp = x_prompt
    xs = x_sample.reshape(bs, D_MODEL)
    outs_p, outs_s = [], []
    for l in range(depth):
        final = l == depth - 1
        w = _layer_weights(w_in[l], w_uq[l], w_uk[l], w_uv[l], w_o[l], w_up[l], conv_w[l], conv_b[l], w_down[l])
        norms = (norm1_g[l].reshape(1, -1), mla_q_norm_g[l].reshape(1, -1), mla_kv_norm_g[l].reshape(1, -1),
                 ret_norm_g[l].reshape(1, -1), norm2_g[l].reshape(1, -1), fg)
        xp, *rest_p = _prompt_layer(xp, w, *norms, btab, tabs, final)
        outs_p.append(rest_p)
        xs, *rest_s = _sample_layer(
            xs, w, *norms, rel_bias, tabs, final,
            cache_moba_k[l].reshape(n_pool, page, A_W), cache_moba_v[l].reshape(n_pool, page, A_W),
            cache_mla[l], state_ret[l], state_conv[l], page_table, past_len)
        outs_s.append(rest_s)
    stack = lambda outs, i: jnp.stack([o[i] for o in outs])
    return (xp, xs.reshape(bs, 1, D_MODEL),
            stack(outs_p, 0), stack(outs_p, 1), stack(outs_p, 2), stack(outs_p, 3), stack(outs_p, 4),
            stack(outs_s, 0), stack(outs_s, 1), stack(outs_s, 2), stack(outs_s, 3), stack(outs_s, 4))
```

```python
import functools
import math

import jax
import jax.numpy as jnp
from jax import lax
from jax.experimental import pallas as pl
from jax.experimental.pallas import tpu as pltpu

F32 = jnp.float32
BF16 = jnp.bfloat16

D_MODEL = 1024
HEAD_DIM = 64
H_A = 6
H_B = 4
H_C = 6
MOBA_BLOCK = 256
MOBA_TOPK = 3
T5_BUCKETS = 32
T5_MAX_DIST = 128
RET_ROPE_BASE = 10000.0
Q_LORA = 256
KV_LORA = 128
D_NOPE = 64
D_ROPE = 32
MLA_ROPE_BASE = 10000.0
CONV_W = 3
NORM_EPS = 1e-6
NEG_INF = -1e30
BELOW_NEG_INF = -3.0e38
LOG2E = 1.4426950408889634
DEN_LANE = KV_LORA + D_ROPE

LANES = 128
A_W = H_A * HEAD_DIM
R_W = H_B * HEAD_DIM
N_IN = 3 * A_W + 4 * R_W + Q_LORA + KV_LORA + D_ROPE
N_IN_PAD = -(-N_IN // LANES) * LANES
M_W = N_IN_PAD - 3 * A_W - 4 * R_W
QCAT_W = 2 * LANES
FFN_CH = 256
VMEM_LIMIT = 56 * 1024 * 1024

_NT = (((1,), (1,)), ((), ()))
_TN = (((0,), (0,)), ((), ()))


def _cparams(sem):
    return pltpu.CompilerParams(dimension_semantics=sem, vmem_limit_bytes=VMEM_LIMIT)


def _rms(x, g):
    return x * lax.rsqrt(jnp.mean(x * x, axis=-1, keepdims=True) + NORM_EPS) * g


def _rope_slab(x, cos, sin_signed, group):
    half = group // 2
    lane = lax.broadcasted_iota(jnp.int32, (1, LANES), 1)
    fwd = pltpu.roll(x, LANES - half, axis=1)
    bwd = pltpu.roll(x, half, axis=1)
    swapped = jnp.where((lane & (group - 1)) < half, fwd, bwd)
    return x * cos + swapped * sin_signed


def _in_proj_kernel(x_ref, g_ref, w_ref, q_ref, k_ref, v_ref, r_ref, m_ref):
    h = _rms(x_ref[...], g_ref[...]).astype(BF16)
    y = jnp.dot(h, w_ref[...], preferred_element_type=F32)
    q_ref[...] = y[:, 0:A_W]
    k_ref[...] = y[:, A_W:2 * A_W]
    v_ref[...] = y[:, 2 * A_W:3 * A_W]
    r_ref[...] = y[:, 3 * A_W:3 * A_W + 4 * R_W]
    m_ref[...] = y[:, 3 * A_W + 4 * R_W:]


def _in_proj(x, g, w):
    m = x.shape[0]
    tm = min(m, 512)
    row = lambda w_: pl.BlockSpec((tm, w_), lambda i: (i, 0))
    full = lambda a: pl.BlockSpec(a.shape, lambda i: (0,) * a.ndim)
    return pl.pallas_call(
        _in_proj_kernel,
        grid=(m // tm,),
        in_specs=[row(D_MODEL), full(g), full(w)],
        out_specs=[row(A_W), row(A_W), row(A_W), row(4 * R_W), row(M_W)],
        out_shape=[jax.ShapeDtypeStruct((m, w_), F32) for w_ in (A_W, A_W, A_W, 4 * R_W, M_W)],
        compiler_params=_cparams(("parallel",)),
        name="in_proj",
    )(x, g, w)


def _mla_prep_kernel(m_ref, gq_ref, gkv_ref, wn_ref, wr_ref, wuk_ref, cos_ref, sin_ref,
                     mla_ref, kcat_ref, qcat_ref):
    scale = (D_NOPE + D_ROPE) ** -0.5 * LOG2E
    m = m_ref[...]
    cos = cos_ref[...]
    sin = sin_ref[...]
    lane = lax.broadcasted_iota(jnp.int32, (1, LANES), 1)
    ckv = _rms(m[:, Q_LORA:Q_LORA + KV_LORA], gkv_ref[...])
    kr = _rope_slab(m[:, Q_LORA + KV_LORA:], cos, sin, D_ROPE)
    mla_ref[:, 0:KV_LORA] = ckv
    mla_ref[:, KV_LORA:] = kr[:, 0:D_ROPE]
    kcat_ref[:, 0:LANES] = ckv.astype(BF16)
    kcat_ref[:, LANES:] = jnp.where(lane < D_ROPE, kr, jnp.where(lane == D_ROPE, 1.0, 0.0)).astype(BF16)
    cq = _rms(m[:, 0:Q_LORA], gq_ref[...]).astype(BF16)
    q_nope = jnp.dot(cq, wn_ref[...], preferred_element_type=F32)
    q_rope = jnp.dot(cq, wr_ref[...], preferred_element_type=F32)
    q_lat = jnp.dot(q_nope.astype(BF16), wuk_ref[...], preferred_element_type=F32)
    per_slab = LANES // D_ROPE
    for h in range(H_C):
        s = h // per_slab
        rot = _rope_slab(q_rope[:, s * LANES:(s + 1) * LANES], cos, sin, D_ROPE)
        sh = (h % per_slab) * D_ROPE
        piece = rot if sh == 0 else pltpu.roll(rot, LANES - sh, axis=1)
        piece = jnp.where(lane < D_ROPE, piece, 0.0) * scale
        qcat_ref[:, h * QCAT_W:h * QCAT_W + LANES] = (q_lat[:, h * LANES:(h + 1) * LANES] * scale).astype(BF16)
        qcat_ref[:, h * QCAT_W + LANES:(h + 1) * QCAT_W] = piece.astype(BF16)


def _mla_prep(m_in, gq, gkv, wn, wr, wuk, cos, sin, rows_per_seq):
    m = m_in.shape[0]
    tm = min(m, 512)
    row = lambda w_: pl.BlockSpec((tm, w_), lambda i: (i, 0))
    full = lambda a: pl.BlockSpec(a.shape, lambda i: (0,) * a.ndim)
    if cos.shape[0] == rows_per_seq and rows_per_seq >= tm:
        nt = rows_per_seq // tm
        tab = pl.BlockSpec((tm, LANES), lambda i: (i % nt, 0))
    else:
        tab = pl.BlockSpec((tm, LANES), lambda i: (0, 0))
    return pl.pallas_call(
        _mla_prep_kernel,
        grid=(m // tm,),
        in_specs=[row(M_W), full(gq), full(gkv), full(wn), full(wr), full(wuk), tab, tab],
        out_specs=[row(KV_LORA + D_ROPE), row(QCAT_W), row(H_C * QCAT_W)],
        out_shape=[jax.ShapeDtypeStruct((m, KV_LORA + D_ROPE), F32),
                   jax.ShapeDtypeStruct((m, QCAT_W), BF16),
                   jax.ShapeDtypeStruct((m, H_C * QCAT_W), BF16)],
        compiler_params=_cparams(("parallel",)),
        name="mla_prep",
    )(m_in, gq, gkv, wn, wr, wuk, cos, sin)


def _mla_attn_kernel(q_ref, k_ref, wuv_ref, o_ref, m_ref, acc_ref, p_ref, s_ref, *, tq, sub):
    qi = pl.program_id(1)
    tiles = [(h, r) for h in range(H_C) for r in range(tq // sub)]

    def keys(j):
        return k_ref[0, pl.ds(pl.multiple_of(j * tq, tq), tq), :]

    def rows(h, r):
        return slice(h * tq + r * sub, h * tq + (r + 1) * sub)

    def scores(h, r, kb):
        return lax.dot_general(q_ref[0, r * sub:(r + 1) * sub, h * QCAT_W:(h + 1) * QCAT_W], kb, _NT,
                               preferred_element_type=F32)

    def visit(t):
        return jnp.where(t == 0, qi, t - 1)

    k_diag = keys(qi)
    col = lax.broadcasted_iota(jnp.int32, (sub, tq), 1)
    row = lax.broadcasted_iota(jnp.int32, (sub, tq), 0)
    for h, r in tiles:
        rs = rows(h, r)
        s_ref[rs] = jnp.where(col <= row + r * sub, scores(h, r, k_diag), NEG_INF)
        m_ref[rs] = jnp.full((sub, LANES), NEG_INF, F32)
        acc_ref[rs] = jnp.zeros((sub, QCAT_W), F32)
        p_ref[rs] = jnp.zeros((sub, tq), BF16)

    def step(t, carry):
        k_next = keys(visit(jnp.minimum(t + 1, qi)))
        k_prev = keys(visit(jnp.maximum(t - 1, 0)))
        for h, r in tiles:
            rs = rows(h, r)
            s_next = scores(h, r, k_next)
            pv = jnp.dot(p_ref[rs], k_prev, preferred_element_type=F32)
            s_cur = s_ref[rs]
            m_i = m_ref[rs]
            m_new = jnp.maximum(m_i, jnp.max(s_cur, axis=-1, keepdims=True))
            alpha = jnp.exp2(m_i - m_new)
            p_ref[rs] = jnp.exp2(s_cur - jnp.concatenate([m_new] * (tq // LANES), axis=1)).astype(BF16)
            acc_ref[rs] = jnp.concatenate([alpha] * (QCAT_W // LANES), axis=1) * (acc_ref[rs] + pv)
            s_ref[rs] = s_next
            m_ref[rs] = m_new
        return carry

    lax.fori_loop(0, qi + 1, step, 0)
    k_last = keys(visit(qi))
    out = jnp.zeros((tq, A_W), F32)
    for h in range(H_C):
        rs = slice(h * tq, (h + 1) * tq)
        acc = acc_ref[rs] + jnp.dot(p_ref[rs], k_last, preferred_element_type=F32)
        o_lat = (acc[:, 0:KV_LORA] / acc[:, DEN_LANE:DEN_LANE + 1]).astype(BF16)
        out = out + jnp.dot(o_lat, wuv_ref[h], preferred_element_type=F32)
    o_ref[0] = out


def _mla_attn(qcat, kcat, wuv_exp):
    b, s, _ = qcat.shape
    tq = min(s, 256)
    sub = min(tq, 128)
    rows = H_C * tq
    return pl.pallas_call(
        functools.partial(_mla_attn_kernel, tq=tq, sub=sub),
        grid=(b, s // tq),
        in_specs=[pl.BlockSpec((1, tq, H_C * QCAT_W), lambda i, j: (i, j, 0)),
                  pl.BlockSpec((1, s, QCAT_W), lambda i, j: (i, 0, 0)),
                  pl.BlockSpec(wuv_exp.shape, lambda i, j: (0, 0, 0))],
        out_specs=pl.BlockSpec((1, tq, A_W), lambda i, j: (i, j, 0)),
        out_shape=jax.ShapeDtypeStruct((b, s, A_W), F32),
        scratch_shapes=[pltpu.VMEM((rows, LANES), F32), pltpu.VMEM((rows, QCAT_W), F32),
                        pltpu.VMEM((rows, tq), BF16), pltpu.VMEM((rows, tq), F32)],
        compiler_params=_cparams(("parallel", "arbitrary")),
        name="mla_attn",
    )(qcat, kcat, wuv_exp)


def _moba_attn_kernel(q_ref, k_ref, v_ref, bt_ref, o_ref, km_ref, kb_ref, vb_ref, qa_ref,
                      m_ref, acc_ref, p_ref, s_ref, *, nb, sub):
    blk = MOBA_BLOCK
    qi = pl.program_id(2)
    scale = HEAD_DIM ** -0.5 * LOG2E
    lane = lax.broadcasted_iota(jnp.int32, (1, LANES), 1)
    lane_f = lane.astype(F32)

    @pl.when(qi == 0)
    def _():
        k = k_ref[0]
        v = v_ref[0]
        km_ref[...] = jnp.zeros_like(km_ref)
        for n in range(nb):
            km_ref[n:n + 1, :] = jnp.mean(k[n * blk:(n + 1) * blk, :], axis=0, keepdims=True)
        row_blk = lax.broadcasted_iota(jnp.int32, k.shape, 0) // blk
        for hh in range(2):
            hmask = (lane // HEAD_DIM) == hh
            other = lane - (1 - hh) * HEAD_DIM
            kb_ref[hh] = jnp.where(hmask, k, jnp.where(other == row_blk, 1.0, 0.0)).astype(BF16)
            vb_ref[hh] = jnp.where(hmask, v, 1.0).astype(BF16)

    q = q_ref[0]
    km = km_ref[...]
    for hh in range(2):
        hmask = (lane // HEAD_DIM) == hh
        qh = jnp.where(hmask, q, 0.0)
        gate = lax.dot_general(qh, km, _NT, precision=lax.Precision.HIGHEST, preferred_element_type=F32)
        gate = jnp.where(lane < qi, gate, NEG_INF)
        picked = jnp.zeros_like(gate)
        for _ in range(MOBA_TOPK):
            mx = jnp.max(gate, axis=-1, keepdims=True)
            idx = jnp.min(jnp.where(gate == mx, lane_f, 1e9), axis=-1, keepdims=True)
            hit = lane_f == idx
            picked = jnp.where(hit, 1.0, picked)
            gate = jnp.where(hit, BELOW_NEG_INF, gate)
        attend = ((picked > 0.0) & (lane < qi)) | (lane == qi)
        pen = jnp.where(attend, 0.0, NEG_INF)
        if hh == 0:
            pen = pltpu.roll(pen, HEAD_DIM, axis=1)
        qa_ref[hh] = jnp.where(hmask, qh * scale, pen).astype(BF16)

    tiles = [(hh, r) for hh in range(2) for r in range(blk // sub)]

    def rows(hh, r):
        return slice(hh * blk + r * sub, hh * blk + (r + 1) * sub)

    def scores(hh, r, j):
        tile = jnp.minimum(qi - j, 2)
        kb = kb_ref[hh, pl.ds(pl.multiple_of(j * blk, blk), blk), :]
        s = lax.dot_general(qa_ref[hh, r * sub:(r + 1) * sub, :], kb, _NT, preferred_element_type=F32)
        return s + bt_ref[hh, tile, r * sub:(r + 1) * sub, :]

    def weighted(hh, p, j):
        return jnp.dot(p, vb_ref[hh, pl.ds(pl.multiple_of(j * blk, blk), blk), :], preferred_element_type=F32)

    def visit(t):
        return jnp.where(t == 0, qi, t - 1)

    for hh, r in tiles:
        rs = rows(hh, r)
        s_ref[rs] = scores(hh, r, qi)
        m_ref[rs] = jnp.full((sub, LANES), NEG_INF, F32)
        acc_ref[rs] = jnp.zeros((sub, LANES), F32)
        p_ref[rs] = jnp.zeros((sub, blk), BF16)

    def step(t, carry):
        j_next = visit(jnp.minimum(t + 1, qi))
        j_prev = visit(jnp.maximum(t - 1, 0))
        for hh, r in tiles:
            rs = rows(hh, r)
            s_next = scores(hh, r, j_next)
            pv = weighted(hh, p_ref[rs], j_prev)
            s_cur = s_ref[rs]
            m_i = m_ref[rs]
            m_new = jnp.maximum(m_i, jnp.max(s_cur, axis=-1, keepdims=True))
            p_ref[rs] = jnp.exp2(s_cur - jnp.concatenate([m_new] * (blk // LANES), axis=1)).astype(BF16)
            acc_ref[rs] = jnp.exp2(m_i - m_new) * (acc_ref[rs] + pv)
            s_ref[rs] = s_next
            m_ref[rs] = m_new
        return carry

    lax.fori_loop(0, qi + 1, step, 0)
    j_last = visit(qi)
    for r in range(blk // sub):
        o = []
        for hh in range(2):
            rs = rows(hh, r)
            acc = acc_ref[rs] + weighted(hh, p_ref[rs], j_last)
            o.append(acc / pltpu.roll(acc, HEAD_DIM, axis=1))
        o_ref[0, r * sub:(r + 1) * sub, :] = jnp.where(lane < HEAD_DIM, o[0], o[1])


def _moba_attn(q, k, v, btab):
    b, s, _ = q.shape
    blk = MOBA_BLOCK
    nb = s // blk
    hp = A_W // LANES
    assert nb <= HEAD_DIM
    sub = 128
    return pl.pallas_call(
        functools.partial(_moba_attn_kernel, nb=nb, sub=sub),
        grid=(b, hp, nb),
        in_specs=[pl.BlockSpec((1, blk, LANES), lambda i, h, j: (i, j, h)),
                  pl.BlockSpec((1, s, LANES), lambda i, h, j: (i, 0, h)),
                  pl.BlockSpec((1, s, LANES), lambda i, h, j: (i, 0, h)),
                  pl.BlockSpec((2, 3, blk, blk), lambda i, h, j: (h, 0, 0, 0))],
        out_specs=pl.BlockSpec((1, blk, LANES), lambda i, h, j: (i, j, h)),
        out_shape=jax.ShapeDtypeStruct((b, s, A_W), F32),
        scratch_shapes=[pltpu.VMEM((LANES, LANES), F32),
                        pltpu.VMEM((2, s, LANES), BF16),
                        pltpu.VMEM((2, s, LANES), BF16),
                        pltpu.VMEM((2, blk, LANES), BF16),
                        pltpu.VMEM((2 * blk, LANES), F32), pltpu.VMEM((2 * blk, LANES), F32),
                        pltpu.VMEM((2 * blk, blk), BF16), pltpu.VMEM((2 * blk, blk), F32)],
        compiler_params=_cparams(("parallel", "parallel", "arbitrary")),
        name="moba_attn",
    )(q, k, v, btab)


def _ret_kernel(r_ref, cos_ref, sin_ref, intra_ref, into_ref, outof_ref, carry_ref, bd_ref, gm_ref,
                gn_ref, o_ref, st_ref, s_scr):
    ci = pl.program_id(1)

    @pl.when(ci == 0)
    def _():
        s_scr[...] = jnp.zeros_like(s_scr)

    r = r_ref[0]
    cos = cos_ref[...]
    sin = sin_ref[...]

    def rope(x):
        return jnp.concatenate(
            [_rope_slab(x[:, s * LANES:(s + 1) * LANES], cos[:, s * LANES:(s + 1) * LANES],
                        sin[:, s * LANES:(s + 1) * LANES], HEAD_DIM) for s in range(R_W // LANES)], axis=1)

    q = rope(r[:, 0:R_W])
    k = rope(r[:, R_W:2 * R_W]) * (HEAD_DIM ** -0.5)
    v = r[:, 2 * R_W:3 * R_W]
    gate = r[:, 3 * R_W:]
    lane_head = lax.broadcasted_iota(jnp.int32, (1, R_W), 1) // HEAD_DIM
    kb = k.astype(BF16)
    state = s_scr[...]
    o = jnp.dot((q * into_ref[...]).astype(BF16), state.astype(BF16), preferred_element_type=F32)
    for h in range(H_B):
        qh = jnp.where(lane_head == h, q, 0.0).astype(BF16)
        a = lax.dot_general(qh, kb, _NT, preferred_element_type=F32) * intra_ref[h]
        vh = jnp.where(lane_head == h, v, 0.0).astype(BF16)
        o = o + jnp.dot(a.astype(BF16), vh, preferred_element_type=F32)
    kv = lax.dot_general((k * outof_ref[...]).astype(BF16), v.astype(BF16), _TN, preferred_element_type=F32)
    new_state = state * carry_ref[...] + kv * bd_ref[...]
    s_scr[...] = new_state
    st_ref[0] = new_state
    ms = jnp.dot(o * o, gm_ref[...], precision=lax.Precision.HIGHEST, preferred_element_type=F32)
    ob = o * lax.rsqrt(ms + NORM_EPS) * gn_ref[...]
    o_ref[0] = ob * (gate * jax.nn.sigmoid(gate))


def _retention_tables(c):
    log_g = jnp.log(1.0 - 2.0 ** (-5.0 - jnp.arange(H_B, dtype=F32)))
    i = jnp.arange(c, dtype=F32)
    diff = i[:, None] - i[None, :]
    intra = jnp.where(diff >= 0, jnp.exp(log_g[:, None, None] * jnp.maximum(diff, 0.0)), 0.0)
    lg_lane = jnp.repeat(log_g, HEAD_DIM)
    into = jnp.exp(lg_lane[None, :] * (i[:, None] + 1.0))
    outof = jnp.exp(lg_lane[None, :] * (c - 1.0 - i[:, None]))
    carry = jnp.exp(lg_lane * c)[:, None]
    head = jnp.arange(R_W) // HEAD_DIM
    bd = (head[:, None] == head[None, :]).astype(F32)
    return intra, into, outof, carry, bd, bd / HEAD_DIM


def _retention(r, cos, sin, gn):
    b, s, _ = r.shape
    c = min(s, 256)
    intra, into, outof, carry, bd, gm = _retention_tables(c)
    full = lambda a: pl.BlockSpec(a.shape, lambda i, j: (0,) * a.ndim)
    tab = pl.BlockSpec((c, R_W), lambda i, j: (j, 0))
    return pl.pallas_call(
        _ret_kernel,
        grid=(b, s // c),
        in_specs=[pl.BlockSpec((1, c, 4 * R_W), lambda i, j: (i, j, 0)), tab, tab,
                  full(intra), full(into), full(outof), full(carry), full(bd), full(gm), full(gn)],
        out_specs=[pl.BlockSpec((1, c, R_W), lambda i, j: (i, j, 0)),
                   pl.BlockSpec((1, R_W, R_W), lambda i, j: (i, 0, 0))],
        out_shape=[jax.ShapeDtypeStruct((b, s, R_W), F32), jax.ShapeDtypeStruct((b, R_W, R_W), F32)],
        scratch_shapes=[pltpu.VMEM((R_W, R_W), F32)],
        compiler_params=_cparams(("parallel", "arbitrary")),
        name="retention",
    )(r, cos, sin, intra, into, outof, carry, bd, gm, gn)


def _ret_step_kernel(q_ref, k_ref, v_ref, g_ref, st_ref, cos_ref, sin_ref, dec_ref, gn_ref, o_ref, ns_ref):
    half = HEAD_DIM // 2
    cos = cos_ref[...]
    sin = sin_ref[...]

    def rope_col(x):
        swapped = jnp.concatenate([x[:, half:, :], x[:, :half, :]], axis=1)
        return x * cos + swapped * sin

    q = rope_col(q_ref[0])
    k = rope_col(k_ref[0]) * (HEAD_DIM ** -0.5)
    v = v_ref[0]
    new_state = st_ref[0] * dec_ref[...] + k * v
    ns_ref[0] = new_state
    o = jnp.sum(q * new_state, axis=1, keepdims=True)
    ob = o * lax.rsqrt(jnp.mean(o * o, axis=-1, keepdims=True) + NORM_EPS) * gn_ref[...]
    gate = g_ref[0]
    o_ref[0] = ob * (gate * jax.nn.sigmoid(gate))


def _retention_step(r, state, pos, gn):
    b = r.shape[0]
    col = lambda x: x.reshape(b, H_B, HEAD_DIM, 1)
    rowv = lambda x: x.reshape(b, H_B, 1, HEAD_DIM)
    half = HEAD_DIM // 2
    inv = RET_ROPE_BASE ** (-jnp.arange(half, dtype=F32) / half)
    ang = jnp.asarray(pos, F32) * inv
    cos = jnp.concatenate([jnp.cos(ang), jnp.cos(ang)]).reshape(1, HEAD_DIM, 1)
    sin = jnp.concatenate([-jnp.sin(ang), jnp.sin(ang)]).reshape(1, HEAD_DIM, 1)
    log_g = jnp.log(1.0 - 2.0 ** (-5.0 - jnp.arange(H_B, dtype=F32)))
    dec = jnp.exp(log_g * 1.0).reshape(H_B, 1, 1)
    gn4 = gn.reshape(H_B, 1, HEAD_DIM)
    cspec = pl.BlockSpec((1, H_B, HEAD_DIM, 1), lambda i: (i, 0, 0, 0))
    rspec = pl.BlockSpec((1, H_B, 1, HEAD_DIM), lambda i: (i, 0, 0, 0))
    sspec = pl.BlockSpec((1, H_B, HEAD_DIM, HEAD_DIM), lambda i: (i, 0, 0, 0))
    full = lambda a: pl.BlockSpec(a.shape, lambda i: (0,) * a.ndim)
    o, ns = pl.pallas_call(
        _ret_step_kernel,
        grid=(b,),
        in_specs=[cspec, cspec, rspec, rspec, sspec, full(cos), full(sin), full(dec), full(gn4)],
        out_specs=[rspec, sspec],
        out_shape=[jax.ShapeDtypeStruct((b, H_B, 1, HEAD_DIM), F32),
                   jax.ShapeDtypeStruct((b, H_B, HEAD_DIM, HEAD_DIM), F32)],
        compiler_params=_cparams(("parallel",)),
        name="retention_step",
    )(col(r[:, 0:R_W]), col(r[:, R_W:2 * R_W]), rowv(r[:, 2 * R_W:3 * R_W]), rowv(r[:, 3 * R_W:]),
      state, cos, sin, dec, gn4)
    return o.reshape(b, R_W), ns


def _out_proj_kernel(x_ref, a_ref, b_ref, c_ref, wa_ref, wb_ref, wc_ref, y_ref):
    y = x_ref[...]
    y = y + jnp.dot(a_ref[...].astype(BF16), wa_ref[...], preferred_element_type=F32)
    y = y + jnp.dot(b_ref[...].astype(BF16), wb_ref[...], preferred_element_type=F32)
    y = y + jnp.dot(c_ref[...].astype(BF16), wc_ref[...], preferred_element_type=F32)
    y_ref[...] = y


def _out_proj(x, oa, ob, oc, wa, wb, wc):
    m = x.shape[0]
    tm = min(m, 512)
    row = lambda w_: pl.BlockSpec((tm, w_), lambda i: (i, 0))
    full = lambda a: pl.BlockSpec(a.shape, lambda i: (0,) * a.ndim)
    return pl.pallas_call(
        _out_proj_kernel,
        grid=(m // tm,),
        in_specs=[row(D_MODEL), row(A_W), row(R_W), row(A_W), full(wa), full(wb), full(wc)],
        out_specs=row(D_MODEL),
        out_shape=jax.ShapeDtypeStruct((m, D_MODEL), F32),
        compiler_params=_cparams(("parallel",)),
        name="out_proj",
    )(x, oa, ob, oc, wa, wb, wc)


_HALO = 16


def _ffn_kernel(xm_ref, xh_ref, st_ref, g2_ref, wup_ref, cw_ref, cb_ref, wdn_ref, fg_ref,
                y_ref, cn_ref, hs_ref, u_ref, act_ref, *, tm, nch, final):
    i = pl.program_id(1)
    g2 = g2_ref[...]
    hs_ref[0:_HALO, :] = _rms(xh_ref[0], g2).astype(BF16)
    hs_ref[_HALO:, :] = _rms(xm_ref[0], g2).astype(BF16)
    y_ref[0] = xm_ref[0]
    lo = _HALO - (CONV_W - 1)
    rc = min(tm, 128)
    row8 = lax.broadcasted_iota(jnp.int32, (8, FFN_CH), 0)
    from_state = (row8 >= 8 - (CONV_W - 1)) & (i == 0)

    def up(c, slot):
        for part in range(2):
            u_ref[slot, part] = jnp.dot(hs_ref[...], wup_ref[part, c], preferred_element_type=F32)
            edge = u_ref[slot, part, _HALO - 8:_HALO, :]
            u_ref[slot, part, _HALO - 8:_HALO, :] = jnp.where(from_state, st_ref[0, part, c], edge)

    def down(c, slot):
        for r0 in range(0, tm, rc):
            convs = []
            for part in range(2):
                w = cw_ref[part, c]
                conv = cb_ref[part, c]
                for t in range(CONV_W):
                    conv = conv + w[t:t + 1, :] * u_ref[slot, part, lo + t + r0:lo + t + r0 + rc, :]
                convs.append(conv)
            act_ref[r0:r0 + rc, :] = ((convs[0] * jax.nn.sigmoid(convs[0])) * convs[1]).astype(BF16)
        for part in range(2):
            cn_ref[0, part, c] = u_ref[slot, part, _HALO + tm - (CONV_W - 1):_HALO + tm, :]
        y_ref[0] += jnp.dot(act_ref[...], wdn_ref[c], preferred_element_type=F32)

    up(0, 0)

    def body(c, carry):
        slot = c & 1
        up(c + 1, 1 - slot)
        down(c, slot)
        return carry

    lax.fori_loop(0, nch - 1, body, 0)
    down(nch - 1, (nch - 1) & 1)
    if final:
        y_ref[0] = _rms(y_ref[0], fg_ref[...])


def _ffn(x, state5, g2, wup, cw, cb, wdn, fg, final):
    b, s, _ = x.shape
    nch = wdn.shape[0]
    tm = min(s, 512)
    nt = s // tm
    hb = tm // _HALO
    full = lambda a: pl.BlockSpec(a.shape, lambda i, j: (0,) * a.ndim)
    state8 = jnp.pad(state5, ((0, 0), (0, 0), (0, 0), (8 - (CONV_W - 1), 0), (0, 0)))
    st_spec = lambda a: pl.BlockSpec((1,) + a.shape[1:], lambda i, j: (i, 0, 0, 0, 0))
    return pl.pallas_call(
        functools.partial(_ffn_kernel, tm=tm, nch=nch, final=final),
        grid=(b, nt),
        in_specs=[pl.BlockSpec((1, tm, D_MODEL), lambda i, j: (i, j, 0)),
                  pl.BlockSpec((1, _HALO, D_MODEL), lambda i, j: (i, jnp.maximum(j * hb - 1, 0), 0)),
                  st_spec(state8), full(g2), full(wup), full(cw), full(cb), full(wdn), full(fg)],
        out_specs=[pl.BlockSpec((1, tm, D_MODEL), lambda i, j: (i, j, 0)), st_spec(state5)],
        out_shape=[jax.ShapeDtypeStruct((b, s, D_MODEL), F32), jax.ShapeDtypeStruct(state5.shape, F32)],
        scratch_shapes=[pltpu.VMEM((tm + _HALO, D_MODEL), BF16),
                        pltpu.VMEM((2, 2, tm + _HALO, FFN_CH), F32),
                        pltpu.VMEM((tm, FFN_CH), BF16)],
        compiler_params=_cparams(("parallel", "arbitrary")),
        name="conv_ffn",
    )(x, x, state8, g2, wup, cw, cb, wdn, fg)


def _ffn_step_kernel(x_ref, st_ref, g2_ref, wup_ref, cw_ref, cb_ref, wdn_ref, fg_ref,
                     y_ref, cn_ref, hs_ref, *, final):
    c = pl.program_id(0)

    @pl.when(c == 0)
    def _():
        hs_ref[...] = _rms(x_ref[...], g2_ref[...]).astype(BF16)
        y_ref[...] = x_ref[...]

    convs = []
    for part in range(2):
        u = jnp.dot(hs_ref[...], wup_ref[part, 0], preferred_element_type=F32)
        w = cw_ref[part, 0]
        s0 = st_ref[0, part, 0]
        s1 = st_ref[0, part, 1]
        convs.append(cb_ref[part, 0] + w[0:1, :] * s0 + w[1:2, :] * s1 + w[2:3, :] * u)
        cn_ref[0, part, 0] = s1
        cn_ref[0, part, 1] = u
    act = (convs[0] * jax.nn.sigmoid(convs[0])) * convs[1]
    y_ref[...] += jnp.dot(act.astype(BF16), wdn_ref[0], preferred_element_type=F32)
    if final:
        @pl.when(c == pl.num_programs(0) - 1)
        def _():
            y_ref[...] = _rms(y_ref[...], fg_ref[...])


def _ffn_step(x, state5, g2, wup, cw, cb, wdn, fg, final):
    b = x.shape[0]
    nch = wdn.shape[0]
    full = lambda a: pl.BlockSpec(a.shape, lambda c: (0,) * a.ndim)
    st_spec = pl.BlockSpec((1,) + state5.shape[1:], lambda c: (c, 0, 0, 0, 0))
    return pl.pallas_call(
        functools.partial(_ffn_step_kernel, final=final),
        grid=(nch,),
        in_specs=[full(x), st_spec, full(g2),
                  pl.BlockSpec((2, 1, D_MODEL, FFN_CH), lambda c: (0, c, 0, 0)),
                  pl.BlockSpec((2, 1, CONV_W, FFN_CH), lambda c: (0, c, 0, 0)),
                  pl.BlockSpec((2, 1, 1, FFN_CH), lambda c: (0, c, 0, 0)),
                  pl.BlockSpec((1, FFN_CH, D_MODEL), lambda c: (c, 0, 0)),
                  full(fg)],
        out_specs=[full(x), st_spec],
        out_shape=[jax.ShapeDtypeStruct((b, D_MODEL), F32), jax.ShapeDtypeStruct(state5.shape, F32)],
        scratch_shapes=[pltpu.VMEM((b, D_MODEL), BF16)],
        compiler_params=_cparams(("arbitrary",)),
        name="conv_ffn_step",
    )(x, state5, g2, wup, cw, cb, wdn, fg)


def _moba_sel_kernel(pt_ref, q_ref, kc_ref, sel_ref, buf, sem, km_ref, *, layer, pc, nbp):
    b = pl.program_id(0)
    nseq = pl.num_programs(0)
    npages = pt_ref.shape[1]
    nchunks = npages // pc
    page = buf.shape[-1]
    ppb = MOBA_BLOCK // page
    bpc = pc // ppb
    lane = lax.broadcasted_iota(jnp.int32, (1, 1, LANES), 2)

    def copies(bb, c, slot):
        return [pltpu.make_async_copy(kc_ref.at[layer, pt_ref[bb, c * pc + p]], buf.at[slot, p], sem.at[slot])
                for p in range(pc)]

    @pl.when(b == 0)
    def _():
        km_ref[...] = jnp.zeros_like(km_ref)
        for cp in copies(0, 0, 0):
            cp.start()

    def chunk(c, carry):
        slot = (b * nchunks + c) & 1

        @pl.when(c + 1 < nchunks)
        def _():
            for cp in copies(b, c + 1, 1 - slot):
                cp.start()

        @pl.when((c + 1 == nchunks) & (b + 1 < nseq))
        def _():
            for cp in copies(b + 1, 0, 1 - slot):
                cp.start()

        for cp in copies(b, c, slot):
            cp.wait()
        km = km_ref[...]
        for n in range(bpc):
            tot = buf[slot, n * ppb]
            for p in range(1, ppb):
                tot = tot + buf[slot, n * ppb + p]
            mean = jnp.sum(tot, axis=-1, keepdims=True) * (1.0 / MOBA_BLOCK)
            km = jnp.where(lane == c * bpc + n, mean, km)
        km_ref[...] = km
        return carry

    lax.fori_loop(0, nchunks, chunk, 0)
    gate = jnp.sum(km_ref[...] * q_ref[0], axis=1, keepdims=True)
    lane_f = lane.astype(F32)
    gate = jnp.where(lane < nbp, gate, BELOW_NEG_INF)
    out = jnp.zeros_like(gate)
    for t in range(MOBA_TOPK):
        mx = jnp.max(gate, axis=-1, keepdims=True)
        idx = jnp.min(jnp.where(gate == mx, lane_f, 1e9), axis=-1, keepdims=True)
        out = jnp.where(lane == t, idx, out)
        gate = jnp.where(lane_f == idx, BELOW_NEG_INF, gate)
    sel_ref[0] = out.astype(jnp.int32)


def _moba_select(page_table, q, kcache_t, layer, nbp):
    b, npages = page_table.shape
    page = kcache_t.shape[-1]
    pc = min(npages, 8)
    assert nbp <= LANES
    grid_spec = pltpu.PrefetchScalarGridSpec(
        num_scalar_prefetch=1,
        grid=(b,),
        in_specs=[pl.BlockSpec((1, H_A, HEAD_DIM, 1), lambda i, pt: (i, 0, 0, 0)),
                  pl.BlockSpec(memory_space=pl.ANY)],
        out_specs=pl.BlockSpec((1, H_A, 1, LANES), lambda i, pt: (i, 0, 0, 0)),
        scratch_shapes=[pltpu.VMEM((2, pc, H_A, HEAD_DIM, page), F32),
                        pltpu.SemaphoreType.DMA((2,)),
                        pltpu.VMEM((H_A, HEAD_DIM, LANES), F32)],
    )
    sel = pl.pallas_call(
        functools.partial(_moba_sel_kernel, layer=layer, pc=pc, nbp=nbp),
        grid_spec=grid_spec,
        out_shape=jax.ShapeDtypeStruct((b, H_A, 1, LANES), jnp.int32),
        compiler_params=_cparams(("arbitrary",)),
        name="moba_select",
    )(page_table, q.reshape(b, H_A, HEAD_DIM, 1), kcache_t)
    return sel[:, :, 0, :MOBA_TOPK]


def _moba_dec_kernel(pt_ref, sel_ref, q_ref, kn_ref, vn_ref, bl_ref, bs_ref, kc_ref, vc_ref, o_ref,
                     kbuf, vbuf, sem, *, layer, nbp, ppb):
    b = pl.program_id(0)
    nseq = pl.num_programs(0)
    scale = HEAD_DIM ** -0.5
    npg = MOBA_TOPK * ppb

    def copies(bb, slot):
        out = []
        for h in range(H_A):
            for t in range(MOBA_TOPK):
                n = sel_ref[bb, h * MOBA_TOPK + t]
                for p in range(ppb):
                    pg = pt_ref[bb, n * ppb + p]
                    out.append(pltpu.make_async_copy(kc_ref.at[layer, pg, h], kbuf.at[slot, h, t * ppb + p],
                                                     sem.at[0, slot]))
                    out.append(pltpu.make_async_copy(vc_ref.at[layer, pg, h], vbuf.at[slot, h, t * ppb + p],
                                                     sem.at[1, slot]))
        return out

    @pl.when(b == 0)
    def _():
        for cp in copies(0, 0):
            cp.start()

    slot = b & 1

    @pl.when(b + 1 < nseq)
    def _():
        for cp in copies(b + 1, 1 - slot):
            cp.start()

    for cp in copies(b, slot):
        cp.wait()

    for h in range(H_A):
        qh = q_ref[0, h] * scale
        far = bs_ref[h, 0]
        rows = []
        for t in range(MOBA_TOPK):
            is_last = sel_ref[b, h * MOBA_TOPK + t] == nbp - 1
            for p in range(ppb):
                s = jnp.sum(kbuf[slot, h, t * ppb + p] * qh, axis=0, keepdims=True)
                rows.append(s + jnp.where(is_last, bl_ref[h, p:p + 1, :], far))
        s_sel = jnp.concatenate(rows, axis=0)
        s_new = jnp.sum(kn_ref[0, h] * qh, axis=0, keepdims=True) + bs_ref[h, 1]
        mx = jnp.maximum(jnp.max(jnp.max(s_sel, axis=1, keepdims=True), axis=0, keepdims=True), s_new)
        p_sel = jnp.exp(s_sel - mx)
        p_new = jnp.exp(s_new - mx)
        denom = jnp.sum(jnp.sum(p_sel, axis=1, keepdims=True), axis=0, keepdims=True) + p_new
        pv = vbuf[slot, h, 0] * p_sel[0:1, :]
        for g in range(1, npg):
            pv = pv + vbuf[slot, h, g] * p_sel[g:g + 1, :]
        o = jnp.sum(pv, axis=1, keepdims=True) + p_new * vn_ref[0, h]
        o_ref[0, h] = o / denom


def _t5_bucket(dist):
    n = jnp.maximum(dist, 0)
    max_exact = T5_BUCKETS // 2
    nf = jnp.maximum(n, 1).astype(F32)
    large = max_exact + (jnp.log(nf / max_exact) / math.log(T5_MAX_DIST / max_exact)
                         * (T5_BUCKETS - max_exact)).astype(jnp.int32)
    large = jnp.minimum(large, T5_BUCKETS - 1)
    return jnp.where(n < max_exact, n, large)


def _moba_decode(page_table, sel, q, k_new, v_new, kcache_t, vcache_t, layer, rel_bias, past_len):
    b, npages = page_table.shape
    page = kcache_t.shape[-1]
    ppb = MOBA_BLOCK // page
    nbp = past_len // MOBA_BLOCK
    off = jnp.arange(MOBA_BLOCK)
    dist_last = past_len - ((nbp - 1) * MOBA_BLOCK + off)
    b_last = _bias_lookup(rel_bias, _t5_bucket(dist_last)).reshape(H_A, ppb, page)
    dist_far = jnp.asarray([past_len - (nbp - 1) * MOBA_BLOCK + 1, 0])
    b_sc = _bias_lookup(rel_bias, _t5_bucket(dist_far))
    sel_flat = sel.reshape(b, H_A * MOBA_TOPK)
    col = lambda x: x.reshape(b, H_A, HEAD_DIM, 1)
    cspec = pl.BlockSpec((1, H_A, HEAD_DIM, 1), lambda i, pt, sl: (i, 0, 0, 0))
    grid_spec = pltpu.PrefetchScalarGridSpec(
        num_scalar_prefetch=2,
        grid=(b,),
        in_specs=[cspec, cspec, cspec,
                  pl.BlockSpec(b_last.shape, lambda i, pt, sl: (0, 0, 0)),
                  pl.BlockSpec(memory_space=pltpu.SMEM),
                  pl.BlockSpec(memory_space=pl.ANY),
                  pl.BlockSpec(memory_space=pl.ANY)],
        out_specs=cspec,
        scratch_shapes=[pltpu.VMEM((2, H_A, MOBA_TOPK * ppb, HEAD_DIM, page), F32),
                        pltpu.VMEM((2, H_A, MOBA_TOPK * ppb, HEAD_DIM, page), F32),
                        pltpu.SemaphoreType.DMA((2, 2))],
    )
    o = pl.pallas_call(
        functools.partial(_moba_dec_kernel, layer=layer, nbp=nbp, ppb=ppb),
        grid_spec=grid_spec,
        out_shape=jax.ShapeDtypeStruct((b, H_A, HEAD_DIM, 1), F32),
        compiler_params=_cparams(("arbitrary",)),
        name="moba_decode",
    )(page_table, sel_flat, col(q), col(k_new), col(v_new), b_last, b_sc, kcache_t, vcache_t)
    return o.reshape(b, A_W)


def _mla_dec_kernel(pt_ref, q_ref, kn_ref, mc_ref, o_ref, buf, sem, *, layer, pc):
    b = pl.program_id(0)
    nseq = pl.num_programs(0)
    npages = pt_ref.shape[1]
    nchunks = npages // pc
    q = q_ref[0]
    q_lat = q[:, 0:KV_LORA]
    q_rope = q[:, KV_LORA:KV_LORA + D_ROPE]

    def copies(bb, c, slot):
        return [pltpu.make_async_copy(mc_ref.at[layer, pt_ref[bb, c * pc + p]], buf.at[slot, p], sem.at[slot])
                for p in range(pc)]

    @pl.when(b == 0)
    def _():
        for cp in copies(0, 0, 0):
            cp.start()

    def chunk(c, carry):
        m_i, l_i, acc = carry
        slot = (b * nchunks + c) & 1

        @pl.when(c + 1 < nchunks)
        def _():
            for cp in copies(b, c + 1, 1 - slot):
                cp.start()

        @pl.when((c + 1 == nchunks) & (b + 1 < nseq))
        def _():
            for cp in copies(b + 1, 0, 1 - slot):
                cp.start()

        for cp in copies(b, c, slot):
            cp.wait()
        kv = jnp.concatenate([buf[slot, p] for p in range(pc)], axis=1).astype(BF16)
        lat = kv[0:KV_LORA]
        s = (jnp.dot(q_lat, lat, preferred_element_type=F32)
             + jnp.dot(q_rope, kv[KV_LORA:], preferred_element_type=F32))
        m_new = jnp.maximum(m_i, jnp.max(s, axis=-1, keepdims=True))
        alpha = jnp.exp2(m_i - m_new)
        p = jnp.exp2(s - m_new)
        l_new = alpha * l_i + jnp.sum(p, axis=-1, keepdims=True)
        acc_new = alpha * acc + lax.dot_general(p.astype(BF16), lat, _NT, preferred_element_type=F32)
        return m_new, l_new, acc_new

    init = (jnp.full((8, 1), NEG_INF, F32), jnp.zeros((8, 1), F32), jnp.zeros((8, KV_LORA), F32))
    m_i, l_i, acc = lax.fori_loop(0, nchunks, chunk, init)
    kn = kn_ref[0].astype(F32)
    s_new = jnp.sum(q.astype(F32) * kn, axis=-1, keepdims=True)
    m_new = jnp.maximum(m_i, s_new)
    alpha = jnp.exp2(m_i - m_new)
    p_new = jnp.exp2(s_new - m_new)
    l_fin = alpha * l_i + p_new
    acc_fin = alpha * acc + p_new * kn[:, 0:KV_LORA]
    o_ref[0] = acc_fin / l_fin


def _mla_decode(page_table, qcat, kcat_new, mcache_t, layer):
    b, npages = page_table.shape
    page = mcache_t.shape[-1]
    pc = min(npages, 16)
    q8 = jnp.pad(qcat.reshape(b, H_C, QCAT_W), ((0, 0), (0, 8 - H_C), (0, 0)))
    grid_spec = pltpu.PrefetchScalarGridSpec(
        num_scalar_prefetch=1,
        grid=(b,),
        in_specs=[pl.BlockSpec((1, 8, QCAT_W), lambda i, pt: (i, 0, 0)),
                  pl.BlockSpec((1, 1, QCAT_W), lambda i, pt: (i, 0, 0)),
                  pl.BlockSpec(memory_space=pl.ANY)],
        out_specs=pl.BlockSpec((1, 8, KV_LORA), lambda i, pt: (i, 0, 0)),
        scratch_shapes=[pltpu.VMEM((2, pc, KV_LORA + D_ROPE, page), F32),
                        pltpu.SemaphoreType.DMA((2,))],
    )
    o_lat = pl.pallas_call(
        functools.partial(_mla_dec_kernel, layer=layer, pc=pc),
        grid_spec=grid_spec,
        out_shape=jax.ShapeDtypeStruct((b, 8, KV_LORA), F32),
        compiler_params=_cparams(("arbitrary",)),
        name="mla_decode",
    )(page_table, q8, kcat_new.reshape(b, 1, QCAT_W), mcache_t)
    return o_lat[:, :H_C, :].reshape(b, H_C * KV_LORA)


def _matmul_kernel(a_ref, w_ref, o_ref):
    o_ref[...] = jnp.dot(a_ref[...].astype(BF16), w_ref[...], preferred_element_type=F32)


def _matmul(a, w):
    full = lambda x: pl.BlockSpec(x.shape, lambda: (0,) * x.ndim)
    return pl.pallas_call(
        _matmul_kernel,
        in_specs=[full(a), full(w)],
        out_specs=pl.BlockSpec((a.shape[0], w.shape[1]), lambda: (0, 0)),
        out_shape=jax.ShapeDtypeStruct((a.shape[0], w.shape[1]), F32),
        name="latent_out",
    )(a, w)


def _rope_tables(pos, base, group, width):
    half = group // 2
    inv = base ** (-jnp.arange(half, dtype=F32) / half)
    ang = pos.astype(F32)[:, None] * inv[None, :]
    cos = jnp.concatenate([jnp.cos(ang), jnp.cos(ang)], axis=1)
    sin = jnp.concatenate([-jnp.sin(ang), jnp.sin(ang)], axis=1)
    reps = width // group
    return jnp.tile(cos, (1, reps)), jnp.tile(sin, (1, reps))


def _bias_lookup(rel_bias, buckets):
    onehot = (buckets[..., None] == jnp.arange(T5_BUCKETS)).astype(F32)
    out = jnp.einsum("...k,kh->...h", onehot, rel_bias.astype(F32), precision=lax.Precision.HIGHEST)
    return jnp.moveaxis(out, -1, 0)


def _moba_bias_tiles(rel_bias):
    r = jnp.arange(MOBA_BLOCK)
    d0 = r[:, None] - r[None, :]
    own = jnp.where((d0 >= 0)[None], _bias_lookup(rel_bias, _t5_bucket(d0)), NEG_INF)
    prev = _bias_lookup(rel_bias, _t5_bucket(d0 + MOBA_BLOCK))
    far = jnp.broadcast_to(_bias_lookup(rel_bias, _t5_bucket(jnp.asarray([MOBA_BLOCK + 1])))[:, :, None], own.shape)
    return jnp.stack([own, prev, far], axis=1) * LOG2E


def _layer_weights(w_in, w_uq, w_uk, w_uv, w_o, w_up, conv_w, conv_b, w_down):
    d_ff = w_down.shape[0]
    nch = d_ff // FFN_CH
    w = {}
    w["in"] = jnp.pad(w_in, ((0, 0), (0, N_IN_PAD - N_IN))).astype(BF16)
    uq = w_uq.reshape(Q_LORA, H_C, D_NOPE + D_ROPE)
    w["uq_nope"] = uq[:, :, :D_NOPE].reshape(Q_LORA, H_C * D_NOPE).astype(BF16)
    w["uq_rope"] = jnp.pad(uq[:, :, D_NOPE:].reshape(Q_LORA, H_C * D_ROPE),
                           ((0, 0), (0, 2 * LANES - H_C * D_ROPE))).astype(BF16)
    eye = jnp.eye(H_C, dtype=F32)
    w["uk_bd"] = jnp.einsum("rhd,hg->hdgr", w_uk, eye).reshape(H_C * D_NOPE, H_C * KV_LORA).astype(BF16)
    w["uv_exp"] = jnp.einsum("rhd,hg->hrgd", w_uv, eye).reshape(H_C, KV_LORA, H_C * HEAD_DIM).astype(BF16)
    w["o_a"] = w_o[0:A_W].astype(BF16)
    w["o_b"] = w_o[A_W:A_W + R_W].astype(BF16)
    w["o_c"] = w_o[A_W + R_W:].astype(BF16)
    w["up"] = w_up.reshape(D_MODEL, 2, nch, FFN_CH).transpose(1, 2, 0, 3).astype(BF16)
    w["conv_w"] = conv_w.reshape(CONV_W, 2, nch, FFN_CH).transpose(1, 2, 0, 3)
    w["conv_b"] = conv_b.reshape(2, nch, 1, FFN_CH)
    w["down"] = w_down.reshape(nch, FFN_CH, D_MODEL).astype(BF16)
    return w


def _state_to_chunks(state, nch):
    b = state.shape[0]
    return state.reshape(b, CONV_W - 1, 2, nch, FFN_CH).transpose(0, 2, 3, 1, 4)


def _chunks_to_state(c5):
    b, _, nch, rows, ch = c5.shape
    return c5.transpose(0, 3, 1, 2, 4).reshape(b, rows, 2 * nch * ch)


def _ret_state_blocks(sfull):
    b = sfull.shape[0]
    s6 = sfull.reshape(b, H_B, HEAD_DIM, H_B, HEAD_DIM)
    return jnp.stack([s6[:, h, :, h, :] for h in range(H_B)], axis=1)


def _prompt_layer(x, w, g1, gq, gkv, gret, g2, fg, btab, tabs, final):
    b, s, _ = x.shape
    m = b * s
    x2 = x.reshape(m, D_MODEL)
    a_q, a_k, a_v, r, mm = _in_proj(x2, g1, w["in"])
    mla_new, kcat, qcat = _mla_prep(mm, gq, gkv, w["uq_nope"], w["uq_rope"], w["uk_bd"],
                                    tabs["mla_cos"], tabs["mla_sin"], s)
    o_a = _moba_attn(a_q.reshape(b, s, A_W), a_k.reshape(b, s, A_W), a_v.reshape(b, s, A_W), btab)
    o_b, sfull = _retention(r.reshape(b, s, 4 * R_W), tabs["ret_cos"], tabs["ret_sin"], gret)
    o_c = _mla_attn(qcat.reshape(b, s, H_C * QCAT_W), kcat.reshape(b, s, QCAT_W), w["uv_exp"])
    x1 = _out_proj(x2, o_a.reshape(m, A_W), o_b.reshape(m, R_W), o_c.reshape(m, A_W), w["o_a"], w["o_b"], w["o_c"])
    nch = w["down"].shape[0]
    st0 = jnp.zeros((b, 2, nch, CONV_W - 1, FFN_CH), F32)
    y, cn = _ffn(x1.reshape(b, s, D_MODEL), st0, g2, w["up"], w["conv_w"], w["conv_b"], w["down"], fg, final)
    return (y, a_k.reshape(b, s, H_A, HEAD_DIM), a_v.reshape(b, s, H_A, HEAD_DIM),
            mla_new.reshape(b, s, KV_LORA + D_ROPE), _ret_state_blocks(sfull), _chunks_to_state(cn))


def _sample_layer(x, w, g1, gq, gkv, gret, g2, fg, rel_bias, tabs, final,
                  kcache_t, vcache_t, mcache_t, layer, ret_state, conv_state, page_table, past_len):
    b = x.shape[0]
    a_q, a_k, a_v, r, mm = _in_proj(x, g1, w["in"])
    mla_new, kcat, qcat = _mla_prep(mm, gq, gkv, w["uq_nope"], w["uq_rope"], w["uk_bd"],
                                    tabs["mla_cos_s"], tabs["mla_sin_s"], 1)
    nbp = past_len // MOBA_BLOCK
    sel = _moba_select(page_table, a_q, kcache_t, layer, nbp)
    o_a = _moba_decode(page_table, sel, a_q, a_k, a_v, kcache_t, vcache_t, layer, rel_bias, past_len)
    o_b, ret_new = _retention_step(r, ret_state, past_len, gret)
    o_lat = _mla_decode(page_table, qcat, kcat, mcache_t, layer)
    o_c = _matmul(o_lat, w["uv_exp"].reshape(H_C * KV_LORA, A_W))
    x1 = _out_proj(x, o_a, o_b, o_c, w["o_a"], w["o_b"], w["o_c"])
    nch = w["down"].shape[0]
    st = _state_to_chunks(conv_state, nch).transpose(2, 1, 3, 0, 4)
    y, cn = _ffn_step(x1, st, g2, w["up"], w["conv_w"], w["conv_b"], w["down"], fg, final)
    conv_new = _chunks_to_state(cn.transpose(3, 1, 0, 2, 4))
    return (y, a_k.reshape(b, 1, H_A, HEAD_DIM), a_v.reshape(b, 1, H_A, HEAD_DIM),
            mla_new.reshape(b, 1, KV_LORA + D_ROPE), ret_new, conv_new)


def kernel(x_prompt, x_sample, cache_moba_k, cache_moba_v, cache_mla, state_ret, state_conv, page_table, norm1_g, w_in, rel_bias, ret_norm_g, mla_q_norm_g, mla_kv_norm_g, w_uq, w_uk, w_uv, w_o, norm2_g, w_up, conv_w, conv_b, w_down, final_norm_g):
    depth = w_in.shape[0]
    bp, s, _ = x_prompt.shape
    bs, s_dec, _ = x_sample.shape
    n_pool, page = cache_moba_k.shape[1], cache_moba_k.shape[2]
    past_len = page_table.shape[1] * page
    assert s % MOBA_BLOCK == 0 and s_dec == 1 and MOBA_BLOCK % page == 0
    assert past_len % MOBA_BLOCK == 0 and past_len // MOBA_BLOCK >= MOBA_TOPK
    assert w_down.shape[1] % FFN_CH == 0

    pos_p = jnp.arange(s)
    pos_s = jnp.full((bs,), past_len)
    tabs = {}
    tabs["ret_cos"], tabs["ret_sin"] = _rope_tables(pos_p, RET_ROPE_BASE, HEAD_DIM, R_W)
    tabs["mla_cos"], tabs["mla_sin"] = _rope_tables(pos_p, MLA_ROPE_BASE, D_ROPE, LANES)
    tabs["mla_cos_s"], tabs["mla_sin_s"] = _rope_tables(pos_s, MLA_ROPE_BASE, D_ROPE, LANES)
    btab = _moba_bias_tiles(rel_bias)
    fg = final_norm_g.reshape(1, D_MODEL)

    kcache_t = jnp.transpose(cache_moba_k, (0, 1, 3, 4, 2))
    vcache_t = jnp.transpose(cache_moba_v, (0, 1, 3, 4, 2))
    mcache_t = jnp.transpose(cache_mla, (0, 1, 3, 2))

    xp = x_prompt
    xs = x_sample.reshape(bs, D_MODEL)
    outs_p, outs_s = [], []
    for l in range(depth):
        final = l == depth - 1
        w = _layer_weights(w_in[l], w_uq[l], w_uk[l], w_uv[l], w_o[l], w_up[l], conv_w[l], conv_b[l], w_down[l])
        norms = (norm1_g[l].reshape(1, -1), mla_q_norm_g[l].reshape(1, -1), mla_kv_norm_g[l].reshape(1, -1),
                 ret_norm_g[l].reshape(1, -1), norm2_g[l].reshape(1, -1), fg)
        xp, *rest_p = _prompt_layer(xp, w, *norms, btab, tabs, final)
        outs_p.append(rest_p)
        xs, *rest_s = _sample_layer(
            xs, w, *norms, rel_bias, tabs, final,
            kcache_t, vcache_t, mcache_t, l, state_ret[l], state_conv[l], page_table, past_len)
        outs_s.append(rest_s)
    stack = lambda outs, i: jnp.stack([o[i] for o in outs])
    return (xp, xs.reshape(bs, 1, D_MODEL),
            stack(outs_p, 0), stack(outs_p, 1), stack(outs_p, 2), stack(outs_p, 3), stack(outs_p, 4),
            stack(outs_s, 0), stack(outs_s, 1), stack(outs_s, 2), stack(outs_s, 3), stack(outs_s, 4))
```

```python
import functools
import math

import jax
import jax.numpy as jnp
from jax import lax
from jax.experimental import pallas as pl
from jax.experimental.pallas import tpu as pltpu

F32 = jnp.float32
BF16 = jnp.bfloat16

D_MODEL = 1024
HEAD_DIM = 64
H_A = 6
H_B = 4
H_C = 6
MOBA_BLOCK = 256
MOBA_TOPK = 3
T5_BUCKETS = 32
T5_MAX_DIST = 128
RET_ROPE_BASE = 10000.0
Q_LORA = 256
KV_LORA = 128
D_NOPE = 64
D_ROPE = 32
MLA_ROPE_BASE = 10000.0
CONV_W = 3
NORM_EPS = 1e-6
NEG_INF = -1e30
BELOW_NEG_INF = -3.0e38
LOG2E = 1.4426950408889634
DEN_LANE = KV_LORA + D_ROPE

LANES = 128
A_W = H_A * HEAD_DIM
R_W = H_B * HEAD_DIM
N_IN = 3 * A_W + 4 * R_W + Q_LORA + KV_LORA + D_ROPE
N_IN_PAD = -(-N_IN // LANES) * LANES
M_W = N_IN_PAD - 3 * A_W - 4 * R_W
QCAT_W = 2 * LANES
FFN_CH = 256
VMEM_LIMIT = 56 * 1024 * 1024
DMA_SLOTS = 3

_NT = (((1,), (1,)), ((), ()))
_TN = (((0,), (0,)), ((), ()))


def _cparams(sem):
    return pltpu.CompilerParams(dimension_semantics=sem, vmem_limit_bytes=VMEM_LIMIT)


def _rms(x, g):
    return x * lax.rsqrt(jnp.mean(x * x, axis=-1, keepdims=True) + NORM_EPS) * g


def _rope_slab(x, cos, sin_signed, group):
    half = group // 2
    lane = lax.broadcasted_iota(jnp.int32, (1, LANES), 1)
    fwd = pltpu.roll(x, LANES - half, axis=1)
    bwd = pltpu.roll(x, half, axis=1)
    swapped = jnp.where((lane & (group - 1)) < half, fwd, bwd)
    return x * cos + swapped * sin_signed


def _in_proj_kernel(x_ref, g_ref, w_ref, q_ref, k_ref, v_ref, r_ref, m_ref):
    h = _rms(x_ref[...], g_ref[...]).astype(BF16)
    y = jnp.dot(h, w_ref[...], preferred_element_type=F32)
    q_ref[...] = y[:, 0:A_W]
    k_ref[...] = y[:, A_W:2 * A_W]
    v_ref[...] = y[:, 2 * A_W:3 * A_W]
    r_ref[...] = y[:, 3 * A_W:3 * A_W + 4 * R_W]
    m_ref[...] = y[:, 3 * A_W + 4 * R_W:]


def _in_proj(x, g, w):
    m = x.shape[0]
    tm = min(m, 512)
    row = lambda w_: pl.BlockSpec((tm, w_), lambda i: (i, 0))
    full = lambda a: pl.BlockSpec(a.shape, lambda i: (0,) * a.ndim)
    return pl.pallas_call(
        _in_proj_kernel,
        grid=(m // tm,),
        in_specs=[row(D_MODEL), full(g), full(w)],
        out_specs=[row(A_W), row(A_W), row(A_W), row(4 * R_W), row(M_W)],
        out_shape=[jax.ShapeDtypeStruct((m, w_), F32) for w_ in (A_W, A_W, A_W, 4 * R_W, M_W)],
        compiler_params=_cparams(("parallel",)),
        name="in_proj",
    )(x, g, w)


def _mla_prep_kernel(m_ref, gq_ref, gkv_ref, wn_ref, wr_ref, wuk_ref, cos_ref, sin_ref,
                     mla_ref, kcat_ref, qcat_ref):
    scale = (D_NOPE + D_ROPE) ** -0.5 * LOG2E
    m = m_ref[...]
    cos = cos_ref[...]
    sin = sin_ref[...]
    lane = lax.broadcasted_iota(jnp.int32, (1, LANES), 1)
    ckv = _rms(m[:, Q_LORA:Q_LORA + KV_LORA], gkv_ref[...])
    kr = _rope_slab(m[:, Q_LORA + KV_LORA:], cos, sin, D_ROPE)
    mla_ref[:, 0:KV_LORA] = ckv
    mla_ref[:, KV_LORA:] = kr[:, 0:D_ROPE]
    kcat_ref[:, 0:LANES] = ckv.astype(BF16)
    kcat_ref[:, LANES:] = jnp.where(lane < D_ROPE, kr, jnp.where(lane == D_ROPE, 1.0, 0.0)).astype(BF16)
    cq = _rms(m[:, 0:Q_LORA], gq_ref[...]).astype(BF16)
    q_nope = jnp.dot(cq, wn_ref[...], preferred_element_type=F32)
    q_rope = jnp.dot(cq, wr_ref[...], preferred_element_type=F32)
    q_lat = jnp.dot(q_nope.astype(BF16), wuk_ref[...], preferred_element_type=F32)
    per_slab = LANES // D_ROPE
    for h in range(H_C):
        s = h // per_slab
        rot = _rope_slab(q_rope[:, s * LANES:(s + 1) * LANES], cos, sin, D_ROPE)
        sh = (h % per_slab) * D_ROPE
        piece = rot if sh == 0 else pltpu.roll(rot, LANES - sh, axis=1)
        piece = jnp.where(lane < D_ROPE, piece, 0.0) * scale
        qcat_ref[:, h * QCAT_W:h * QCAT_W + LANES] = (q_lat[:, h * LANES:(h + 1) * LANES] * scale).astype(BF16)
        qcat_ref[:, h * QCAT_W + LANES:(h + 1) * QCAT_W] = piece.astype(BF16)


def _mla_prep(m_in, gq, gkv, wn, wr, wuk, cos, sin, rows_per_seq):
    m = m_in.shape[0]
    tm = min(m, 512)
    row = lambda w_: pl.BlockSpec((tm, w_), lambda i: (i, 0))
    full = lambda a: pl.BlockSpec(a.shape, lambda i: (0,) * a.ndim)
    if cos.shape[0] == rows_per_seq and rows_per_seq >= tm:
        nt = rows_per_seq // tm
        tab = pl.BlockSpec((tm, LANES), lambda i: (i % nt, 0))
    else:
        tab = pl.BlockSpec((tm, LANES), lambda i: (0, 0))
    return pl.pallas_call(
        _mla_prep_kernel,
        grid=(m // tm,),
        in_specs=[row(M_W), full(gq), full(gkv), full(wn), full(wr), full(wuk), tab, tab],
        out_specs=[row(KV_LORA + D_ROPE), row(QCAT_W), row(H_C * QCAT_W)],
        out_shape=[jax.ShapeDtypeStruct((m, KV_LORA + D_ROPE), F32),
                   jax.ShapeDtypeStruct((m, QCAT_W), BF16),
                   jax.ShapeDtypeStruct((m, H_C * QCAT_W), BF16)],
        compiler_params=_cparams(("parallel",)),
        name="mla_prep",
    )(m_in, gq, gkv, wn, wr, wuk, cos, sin)


def _mla_attn_kernel(q_ref, k_ref, wuv_ref, o_ref, m_ref, acc_ref, p_ref, s_ref, *, tq, sub):
    qi = pl.program_id(1)
    tiles = [(h, r) for h in range(H_C) for r in range(tq // sub)]

    def keys(j):
        return k_ref[0, pl.ds(pl.multiple_of(j * tq, tq), tq), :]

    def rows(h, r):
        return slice(h * tq + r * sub, h * tq + (r + 1) * sub)

    def scores(h, r, kb):
        return lax.dot_general(q_ref[0, r * sub:(r + 1) * sub, h * QCAT_W:(h + 1) * QCAT_W], kb, _NT,
                               preferred_element_type=F32)

    def visit(t):
        return jnp.where(t == 0, qi, t - 1)

    k_diag = keys(qi)
    col = lax.broadcasted_iota(jnp.int32, (sub, tq), 1)
    row = lax.broadcasted_iota(jnp.int32, (sub, tq), 0)
    for h, r in tiles:
        rs = rows(h, r)
        s_ref[rs] = jnp.where(col <= row + r * sub, scores(h, r, k_diag), NEG_INF)
        m_ref[rs] = jnp.full((sub, LANES), NEG_INF, F32)
        acc_ref[rs] = jnp.zeros((sub, QCAT_W), F32)
        p_ref[rs] = jnp.zeros((sub, tq), BF16)

    def step(t, carry):
        k_next = keys(visit(jnp.minimum(t + 1, qi)))
        k_prev = keys(visit(jnp.maximum(t - 1, 0)))
        for h, r in tiles:
            rs = rows(h, r)
            s_next = scores(h, r, k_next)
            pv = jnp.dot(p_ref[rs], k_prev, preferred_element_type=F32)
            s_cur = s_ref[rs]
            m_i = m_ref[rs]
            m_new = jnp.maximum(m_i, jnp.max(s_cur, axis=-1, keepdims=True))
            alpha = jnp.exp2(m_i - m_new)
            p_ref[rs] = jnp.exp2(s_cur - jnp.concatenate([m_new] * (tq // LANES), axis=1)).astype(BF16)
            acc_ref[rs] = jnp.concatenate([alpha] * (QCAT_W // LANES), axis=1) * (acc_ref[rs] + pv)
            s_ref[rs] = s_next
            m_ref[rs] = m_new
        return carry

    lax.fori_loop(0, qi + 1, step, 0)
    k_last = keys(visit(qi))
    out = jnp.zeros((tq, A_W), F32)
    for h in range(H_C):
        rs = slice(h * tq, (h + 1) * tq)
        acc = acc_ref[rs] + jnp.dot(p_ref[rs], k_last, preferred_element_type=F32)
        o_lat = (acc[:, 0:KV_LORA] / acc[:, DEN_LANE:DEN_LANE + 1]).astype(BF16)
        out = out + jnp.dot(o_lat, wuv_ref[h], preferred_element_type=F32)
    o_ref[0] = out


def _mla_attn(qcat, kcat, wuv_exp):
    b, s, _ = qcat.shape
    tq = min(s, 256)
    sub = min(tq, 256)
    rows = H_C * tq
    return pl.pallas_call(
        functools.partial(_mla_attn_kernel, tq=tq, sub=sub),
        grid=(b, s // tq),
        in_specs=[pl.BlockSpec((1, tq, H_C * QCAT_W), lambda i, j: (i, j, 0)),
                  pl.BlockSpec((1, s, QCAT_W), lambda i, j: (i, 0, 0)),
                  pl.BlockSpec(wuv_exp.shape, lambda i, j: (0, 0, 0))],
        out_specs=pl.BlockSpec((1, tq, A_W), lambda i, j: (i, j, 0)),
        out_shape=jax.ShapeDtypeStruct((b, s, A_W), F32),
        scratch_shapes=[pltpu.VMEM((rows, LANES), F32), pltpu.VMEM((rows, QCAT_W), F32),
                        pltpu.VMEM((rows, tq), BF16), pltpu.VMEM((rows, tq), F32)],
        compiler_params=_cparams(("parallel", "arbitrary")),
        name="mla_attn",
    )(qcat, kcat, wuv_exp)


def _moba_attn_kernel(q_ref, k_ref, v_ref, bt_ref, o_ref, km_ref, kb_ref, vb_ref, qa_ref,
                      m_ref, acc_ref, p_ref, s_ref, *, nb, sub):
    blk = MOBA_BLOCK
    qi = pl.program_id(2)
    scale = HEAD_DIM ** -0.5 * LOG2E
    lane = lax.broadcasted_iota(jnp.int32, (1, LANES), 1)
    lane_f = lane.astype(F32)

    @pl.when(qi == 0)
    def _():
        k = k_ref[0]
        v = v_ref[0]
        km_ref[...] = jnp.zeros_like(km_ref)
        for n in range(nb):
            km_ref[n:n + 1, :] = jnp.mean(k[n * blk:(n + 1) * blk, :], axis=0, keepdims=True)
        row_blk = lax.broadcasted_iota(jnp.int32, k.shape, 0) // blk
        for hh in range(2):
            hmask = (lane // HEAD_DIM) == hh
            other = lane - (1 - hh) * HEAD_DIM
            kb_ref[hh] = jnp.where(hmask, k, jnp.where(other == row_blk, 1.0, 0.0)).astype(BF16)
            vb_ref[hh] = jnp.where(hmask, v, 1.0).astype(BF16)

    q = q_ref[0]
    km = km_ref[...]
    for hh in range(2):
        hmask = (lane // HEAD_DIM) == hh
        qh = jnp.where(hmask, q, 0.0)
        gate = lax.dot_general(qh, km, _NT, precision=lax.Precision.HIGHEST, preferred_element_type=F32)
        gate = jnp.where(lane < qi, gate, NEG_INF)
        picked = jnp.zeros_like(gate)
        for _ in range(MOBA_TOPK):
            mx = jnp.max(gate, axis=-1, keepdims=True)
            idx = jnp.min(jnp.where(gate == mx, lane_f, 1e9), axis=-1, keepdims=True)
            hit = lane_f == idx
            picked = jnp.where(hit, 1.0, picked)
            gate = jnp.where(hit, BELOW_NEG_INF, gate)
        attend = ((picked > 0.0) & (lane < qi)) | (lane == qi)
        pen = jnp.where(attend, 0.0, NEG_INF)
        if hh == 0:
            pen = pltpu.roll(pen, HEAD_DIM, axis=1)
        qa_ref[hh] = jnp.where(hmask, qh * scale, pen).astype(BF16)

    tiles = [(hh, r) for hh in range(2) for r in range(blk // sub)]

    def rows(hh, r):
        return slice(hh * blk + r * sub, hh * blk + (r + 1) * sub)

    def scores(hh, r, j):
        tile = jnp.minimum(qi - j, 2)
        kb = kb_ref[hh, pl.ds(pl.multiple_of(j * blk, blk), blk), :]
        s = lax.dot_general(qa_ref[hh, r * sub:(r + 1) * sub, :], kb, _NT, preferred_element_type=F32)
        return s + bt_ref[hh, tile, r * sub:(r + 1) * sub, :]

    def weighted(hh, p, j):
        return jnp.dot(p, vb_ref[hh, pl.ds(pl.multiple_of(j * blk, blk), blk), :], preferred_element_type=F32)

    def visit(t):
        return jnp.where(t == 0, qi, t - 1)

    for hh, r in tiles:
        rs = rows(hh, r)
        s_ref[rs] = scores(hh, r, qi)
        m_ref[rs] = jnp.full((sub, LANES), NEG_INF, F32)
        acc_ref[rs] = jnp.zeros((sub, LANES), F32)
        p_ref[rs] = jnp.zeros((sub, blk), BF16)

    def step(t, carry):
        j_next = visit(jnp.minimum(t + 1, qi))
        j_prev = visit(jnp.maximum(t - 1, 0))
        for hh, r in tiles:
            rs = rows(hh, r)
            s_next = scores(hh, r, j_next)
            pv = weighted(hh, p_ref[rs], j_prev)
            s_cur = s_ref[rs]
            m_i = m_ref[rs]
            m_new = jnp.maximum(m_i, jnp.max(s_cur, axis=-1, keepdims=True))
            p_ref[rs] = jnp.exp2(s_cur - jnp.concatenate([m_new] * (blk // LANES), axis=1)).astype(BF16)
            acc_ref[rs] = jnp.exp2(m_i - m_new) * (acc_ref[rs] + pv)
            s_ref[rs] = s_next
            m_ref[rs] = m_new
        return carry

    lax.fori_loop(0, qi + 1, step, 0)
    j_last = visit(qi)
    for r in range(blk // sub):
        o = []
        for hh in range(2):
            rs = rows(hh, r)
            acc = acc_ref[rs] + weighted(hh, p_ref[rs], j_last)
            o.append(acc / pltpu.roll(acc, HEAD_DIM, axis=1))
        o_ref[0, r * sub:(r + 1) * sub, :] = jnp.where(lane < HEAD_DIM, o[0], o[1])


def _moba_attn(q, k, v, btab):
    b, s, _ = q.shape
    blk = MOBA_BLOCK
    nb = s // blk
    hp = A_W // LANES
    assert nb <= HEAD_DIM
    sub = 256
    return pl.pallas_call(
        functools.partial(_moba_attn_kernel, nb=nb, sub=sub),
        grid=(b, hp, nb),
        in_specs=[pl.BlockSpec((1, blk, LANES), lambda i, h, j: (i, j, h)),
                  pl.BlockSpec((1, s, LANES), lambda i, h, j: (i, 0, h)),
                  pl.BlockSpec((1, s, LANES), lambda i, h, j: (i, 0, h)),
                  pl.BlockSpec((2, 3, blk, blk), lambda i, h, j: (h, 0, 0, 0))],
        out_specs=pl.BlockSpec((1, blk, LANES), lambda i, h, j: (i, j, h)),
        out_shape=jax.ShapeDtypeStruct((b, s, A_W), F32),
        scratch_shapes=[pltpu.VMEM((LANES, LANES), F32),
                        pltpu.VMEM((2, s, LANES), BF16),
                        pltpu.VMEM((2, s, LANES), BF16),
                        pltpu.VMEM((2, blk, LANES), BF16),
                        pltpu.VMEM((2 * blk, LANES), F32), pltpu.VMEM((2 * blk, LANES), F32),
                        pltpu.VMEM((2 * blk, blk), BF16), pltpu.VMEM((2 * blk, blk), F32)],
        compiler_params=_cparams(("parallel", "parallel", "arbitrary")),
        name="moba_attn",
    )(q, k, v, btab)


def _ret_kernel(r_ref, cos_ref, sin_ref, intra_ref, into_ref, outof_ref, carry_ref, bd_ref, gm_ref,
                gn_ref, o_ref, st_ref, s_scr):
    ci = pl.program_id(1)

    @pl.when(ci == 0)
    def _():
        s_scr[...] = jnp.zeros_like(s_scr)

    r = r_ref[0]
    cos = cos_ref[...]
    sin = sin_ref[...]

    def rope(x):
        return jnp.concatenate(
            [_rope_slab(x[:, s * LANES:(s + 1) * LANES], cos[:, s * LANES:(s + 1) * LANES],
                        sin[:, s * LANES:(s + 1) * LANES], HEAD_DIM) for s in range(R_W // LANES)], axis=1)

    q = rope(r[:, 0:R_W])
    k = rope(r[:, R_W:2 * R_W]) * (HEAD_DIM ** -0.5)
    v = r[:, 2 * R_W:3 * R_W]
    gate = r[:, 3 * R_W:]
    lane_head = lax.broadcasted_iota(jnp.int32, (1, R_W), 1) // HEAD_DIM
    kb = k.astype(BF16)
    state = s_scr[...]
    o = jnp.dot((q * into_ref[...]).astype(BF16), state.astype(BF16), preferred_element_type=F32)
    for h in range(H_B):
        qh = jnp.where(lane_head == h, q, 0.0).astype(BF16)
        a = lax.dot_general(qh, kb, _NT, preferred_element_type=F32) * intra_ref[h]
        vh = jnp.where(lane_head == h, v, 0.0).astype(BF16)
        o = o + jnp.dot(a.astype(BF16), vh, preferred_element_type=F32)
    kv = lax.dot_general((k * outof_ref[...]).astype(BF16), v.astype(BF16), _TN, preferred_element_type=F32)
    new_state = state * carry_ref[...] + kv * bd_ref[...]
    s_scr[...] = new_state
    st_ref[0] = new_state
    ms = jnp.dot(o * o, gm_ref[...], precision=lax.Precision.HIGHEST, preferred_element_type=F32)
    ob = o * lax.rsqrt(ms + NORM_EPS) * gn_ref[...]
    o_ref[0] = ob * (gate * jax.nn.sigmoid(gate))


def _retention_tables(c):
    log_g = jnp.log(1.0 - 2.0 ** (-5.0 - jnp.arange(H_B, dtype=F32)))
    i = jnp.arange(c, dtype=F32)
    diff = i[:, None] - i[None, :]
    intra = jnp.where(diff >= 0, jnp.exp(log_g[:, None, None] * jnp.maximum(diff, 0.0)), 0.0)
    lg_lane = jnp.repeat(log_g, HEAD_DIM)
    into = jnp.exp(lg_lane[None, :] * (i[:, None] + 1.0))
    outof = jnp.exp(lg_lane[None, :] * (c - 1.0 - i[:, None]))
    carry = jnp.exp(lg_lane * c)[:, None]
    head = jnp.arange(R_W) // HEAD_DIM
    bd = (head[:, None] == head[None, :]).astype(F32)
    return intra, into, outof, carry, bd, bd / HEAD_DIM


def _retention(r, cos, sin, gn):
    b, s, _ = r.shape
    c = min(s, 256)
    intra, into, outof, carry, bd, gm = _retention_tables(c)
    full = lambda a: pl.BlockSpec(a.shape, lambda i, j: (0,) * a.ndim)
    tab = pl.BlockSpec((c, R_W), lambda i, j: (j, 0))
    return pl.pallas_call(
        _ret_kernel,
        grid=(b, s // c),
        in_specs=[pl.BlockSpec((1, c, 4 * R_W), lambda i, j: (i, j, 0)), tab, tab,
                  full(intra), full(into), full(outof), full(carry), full(bd), full(gm), full(gn)],
        out_specs=[pl.BlockSpec((1, c, R_W), lambda i, j: (i, j, 0)),
                   pl.BlockSpec((1, R_W, R_W), lambda i, j: (i, 0, 0))],
        out_shape=[jax.ShapeDtypeStruct((b, s, R_W), F32), jax.ShapeDtypeStruct((b, R_W, R_W), F32)],
        scratch_shapes=[pltpu.VMEM((R_W, R_W), F32)],
        compiler_params=_cparams(("parallel", "arbitrary")),
        name="retention",
    )(r, cos, sin, intra, into, outof, carry, bd, gm, gn)


def _ret_step_kernel(q_ref, k_ref, v_ref, g_ref, st_ref, cos_ref, sin_ref, dec_ref, gn_ref, o_ref, ns_ref):
    half = HEAD_DIM // 2
    cos = cos_ref[...]
    sin = sin_ref[...]

    def rope_col(x):
        swapped = jnp.concatenate([x[:, half:, :], x[:, :half, :]], axis=1)
        return x * cos + swapped * sin

    q = rope_col(q_ref[0])
    k = rope_col(k_ref[0]) * (HEAD_DIM ** -0.5)
    v = v_ref[0]
    new_state = st_ref[0] * dec_ref[...] + k * v
    ns_ref[0] = new_state
    o = jnp.sum(q * new_state, axis=1, keepdims=True)
    ob = o * lax.rsqrt(jnp.mean(o * o, axis=-1, keepdims=True) + NORM_EPS) * gn_ref[...]
    gate = g_ref[0]
    o_ref[0] = ob * (gate * jax.nn.sigmoid(gate))


def _retention_step(r, state, pos, gn):
    b = r.shape[0]
    col = lambda x: x.reshape(b, H_B, HEAD_DIM, 1)
    rowv = lambda x: x.reshape(b, H_B, 1, HEAD_DIM)
    half = HEAD_DIM // 2
    inv = RET_ROPE_BASE ** (-jnp.arange(half, dtype=F32) / half)
    ang = jnp.asarray(pos, F32) * inv
    cos = jnp.concatenate([jnp.cos(ang), jnp.cos(ang)]).reshape(1, HEAD_DIM, 1)
    sin = jnp.concatenate([-jnp.sin(ang), jnp.sin(ang)]).reshape(1, HEAD_DIM, 1)
    log_g = jnp.log(1.0 - 2.0 ** (-5.0 - jnp.arange(H_B, dtype=F32)))
    dec = jnp.exp(log_g * 1.0).reshape(H_B, 1, 1)
    gn4 = gn.reshape(H_B, 1, HEAD_DIM)
    cspec = pl.BlockSpec((1, H_B, HEAD_DIM, 1), lambda i: (i, 0, 0, 0))
    rspec = pl.BlockSpec((1, H_B, 1, HEAD_DIM), lambda i: (i, 0, 0, 0))
    sspec = pl.BlockSpec((1, H_B, HEAD_DIM, HEAD_DIM), lambda i: (i, 0, 0, 0))
    full = lambda a: pl.BlockSpec(a.shape, lambda i: (0,) * a.ndim)
    o, ns = pl.pallas_call(
        _ret_step_kernel,
        grid=(b,),
        in_specs=[cspec, cspec, rspec, rspec, sspec, full(cos), full(sin), full(dec), full(gn4)],
        out_specs=[rspec, sspec],
        out_shape=[jax.ShapeDtypeStruct((b, H_B, 1, HEAD_DIM), F32),
                   jax.ShapeDtypeStruct((b, H_B, HEAD_DIM, HEAD_DIM), F32)],
        compiler_params=_cparams(("parallel",)),
        name="retention_step",
    )(col(r[:, 0:R_W]), col(r[:, R_W:2 * R_W]), rowv(r[:, 2 * R_W:3 * R_W]), rowv(r[:, 3 * R_W:]),
      state, cos, sin, dec, gn4)
    return o.reshape(b, R_W), ns


def _out_proj_kernel(x_ref, a_ref, b_ref, c_ref, wa_ref, wb_ref, wc_ref, y_ref):
    y = x_ref[...]
    y = y + jnp.dot(a_ref[...].astype(BF16), wa_ref[...], preferred_element_type=F32)
    y = y + jnp.dot(b_ref[...].astype(BF16), wb_ref[...], preferred_element_type=F32)
    y = y + jnp.dot(c_ref[...].astype(BF16), wc_ref[...], preferred_element_type=F32)
    y_ref[...] = y


def _out_proj(x, oa, ob, oc, wa, wb, wc):
    m = x.shape[0]
    tm = min(m, 512)
    row = lambda w_: pl.BlockSpec((tm, w_), lambda i: (i, 0))
    full = lambda a: pl.BlockSpec(a.shape, lambda i: (0,) * a.ndim)
    return pl.pallas_call(
        _out_proj_kernel,
        grid=(m // tm,),
        in_specs=[row(D_MODEL), row(A_W), row(R_W), row(A_W), full(wa), full(wb), full(wc)],
        out_specs=row(D_MODEL),
        out_shape=jax.ShapeDtypeStruct((m, D_MODEL), F32),
        compiler_params=_cparams(("parallel",)),
        name="out_proj",
    )(x, oa, ob, oc, wa, wb, wc)


_HALO = 16


def _ffn_kernel(xm_ref, xh_ref, st_ref, g2_ref, wup_ref, cw_ref, cb_ref, wdn_ref, fg_ref,
                y_ref, cn_ref, hs_ref, u_ref, act_ref, *, tm, nch, final):
    i = pl.program_id(1)
    g2 = g2_ref[...]
    hs_ref[0:_HALO, :] = _rms(xh_ref[0], g2).astype(BF16)
    hs_ref[_HALO:, :] = _rms(xm_ref[0], g2).astype(BF16)
    lo = _HALO - (CONV_W - 1)
    rc = min(tm, 128)
    row8 = lax.broadcasted_iota(jnp.int32, (8, FFN_CH), 0)
    from_state = (row8 >= 8 - (CONV_W - 1)) & (i == 0)

    def up(c, slot):
        for part in range(2):
            u_ref[slot, part] = jnp.dot(hs_ref[...], wup_ref[part, c], preferred_element_type=F32)
            edge = u_ref[slot, part, _HALO - 8:_HALO, :]
            u_ref[slot, part, _HALO - 8:_HALO, :] = jnp.where(from_state, st_ref[0, part, c], edge)

    def gate(c, slot):
        for r0 in range(0, tm, rc):
            convs = []
            for part in range(2):
                w = cw_ref[part, c]
                conv = cb_ref[part, c]
                for t in range(CONV_W):
                    conv = conv + w[t:t + 1, :] * u_ref[slot, part, lo + t + r0:lo + t + r0 + rc, :]
                convs.append(conv)
            act_ref[r0:r0 + rc, c * FFN_CH:(c + 1) * FFN_CH] = (
                (convs[0] * jax.nn.sigmoid(convs[0])) * convs[1]).astype(BF16)
        for part in range(2):
            cn_ref[0, part, c] = u_ref[slot, part, _HALO + tm - (CONV_W - 1):_HALO + tm, :]

    up(0, 0)
    for c in range(nch):
        if c + 1 < nch:
            up(c + 1, (c + 1) & 1)
        gate(c, c & 1)
    y = xm_ref[0] + jnp.dot(act_ref[...], wdn_ref[...], preferred_element_type=F32)
    y_ref[0] = _rms(y, fg_ref[...]) if final else y


def _ffn(x, state5, g2, wup, cw, cb, wdn, fg, final):
    b, s, _ = x.shape
    nch = wdn.shape[0]
    tm = min(s, 512)
    nt = s // tm
    hb = tm // _HALO
    full = lambda a: pl.BlockSpec(a.shape, lambda i, j: (0,) * a.ndim)
    state8 = jnp.pad(state5, ((0, 0), (0, 0), (0, 0), (8 - (CONV_W - 1), 0), (0, 0)))
    st_spec = lambda a: pl.BlockSpec((1,) + a.shape[1:], lambda i, j: (i, 0, 0, 0, 0))
    wdn = wdn.reshape(nch * FFN_CH, D_MODEL)
    return pl.pallas_call(
        functools.partial(_ffn_kernel, tm=tm, nch=nch, final=final),
        grid=(b, nt),
        in_specs=[pl.BlockSpec((1, tm, D_MODEL), lambda i, j: (i, j, 0)),
                  pl.BlockSpec((1, _HALO, D_MODEL), lambda i, j: (i, jnp.maximum(j * hb - 1, 0), 0)),
                  st_spec(state8), full(g2), full(wup), full(cw), full(cb), full(wdn), full(fg)],
        out_specs=[pl.BlockSpec((1, tm, D_MODEL), lambda i, j: (i, j, 0)), st_spec(state5)],
        out_shape=[jax.ShapeDtypeStruct((b, s, D_MODEL), F32), jax.ShapeDtypeStruct(state5.shape, F32)],
        scratch_shapes=[pltpu.VMEM((tm + _HALO, D_MODEL), BF16),
                        pltpu.VMEM((2, 2, tm + _HALO, FFN_CH), F32),
                        pltpu.VMEM((tm, nch * FFN_CH), BF16)],
        compiler_params=_cparams(("parallel", "arbitrary")),
        name="conv_ffn",
    )(x, x, state8, g2, wup, cw, cb, wdn, fg)


def _ffn_step_kernel(x_ref, st_ref, g2_ref, wup_ref, cw_ref, cb_ref, wdn_ref, fg_ref,
                     y_ref, cn_ref, hs_ref, *, final):
    c = pl.program_id(0)

    @pl.when(c == 0)
    def _():
        hs_ref[...] = _rms(x_ref[...], g2_ref[...]).astype(BF16)
        y_ref[...] = x_ref[...]

    convs = []
    for part in range(2):
        u = jnp.dot(hs_ref[...], wup_ref[part, 0], preferred_element_type=F32)
        w = cw_ref[part, 0]
        s0 = st_ref[0, part, 0]
        s1 = st_ref[0, part, 1]
        convs.append(cb_ref[part, 0] + w[0:1, :] * s0 + w[1:2, :] * s1 + w[2:3, :] * u)
        cn_ref[0, part, 0] = s1
        cn_ref[0, part, 1] = u
    act = (convs[0] * jax.nn.sigmoid(convs[0])) * convs[1]
    y_ref[...] += jnp.dot(act.astype(BF16), wdn_ref[0], preferred_element_type=F32)
    if final:
        @pl.when(c == pl.num_programs(0) - 1)
        def _():
            y_ref[...] = _rms(y_ref[...], fg_ref[...])


def _ffn_step(x, state5, g2, wup, cw, cb, wdn, fg, final):
    b = x.shape[0]
    nch = wdn.shape[0]
    full = lambda a: pl.BlockSpec(a.shape, lambda c: (0,) * a.ndim)
    st_spec = pl.BlockSpec((1,) + state5.shape[1:], lambda c: (c, 0, 0, 0, 0))
    return pl.pallas_call(
        functools.partial(_ffn_step_kernel, final=final),
        grid=(nch,),
        in_specs=[full(x), st_spec, full(g2),
                  pl.BlockSpec((2, 1, D_MODEL, FFN_CH), lambda c: (0, c, 0, 0)),
                  pl.BlockSpec((2, 1, CONV_W, FFN_CH), lambda c: (0, c, 0, 0)),
                  pl.BlockSpec((2, 1, 1, FFN_CH), lambda c: (0, c, 0, 0)),
                  pl.BlockSpec((1, FFN_CH, D_MODEL), lambda c: (c, 0, 0)),
                  full(fg)],
        out_specs=[full(x), st_spec],
        out_shape=[jax.ShapeDtypeStruct((b, D_MODEL), F32), jax.ShapeDtypeStruct(state5.shape, F32)],
        scratch_shapes=[pltpu.VMEM((b, D_MODEL), BF16)],
        compiler_params=_cparams(("arbitrary",)),
        name="conv_ffn_step",
    )(x, state5, g2, wup, cw, cb, wdn, fg)


def _moba_sel_kernel(pt_ref, q_ref, kc_ref, sel_ref, buf, sem, km_ref, *, layer, pc, nbp):
    b = pl.program_id(0)
    nseq = pl.num_programs(0)
    npages = pt_ref.shape[1]
    nchunks = npages // pc
    page = buf.shape[-1]
    ppb = MOBA_BLOCK // page
    bpc = pc // ppb
    lane = lax.broadcasted_iota(jnp.int32, (1, 1, LANES), 2)

    nslots = buf.shape[0]
    total = nseq * nchunks

    def copies(g):
        bb = g // nchunks
        c = g % nchunks
        slot = g % nslots
        return [pltpu.make_async_copy(kc_ref.at[layer, pt_ref[bb, c * pc + p]], buf.at[slot, p], sem.at[slot])
                for p in range(pc)]

    def start(g):
        for p, cp in enumerate(copies(g)):
            cp.start(priority=p % 2)

    @pl.when(b == 0)
    def _():
        km_ref[...] = jnp.zeros_like(km_ref)
        for g in range(nslots - 1):
            start(g)

    def chunk(c, carry):
        g = b * nchunks + c
        slot = g % nslots

        @pl.when(g + nslots - 1 < total)
        def _():
            start(g + nslots - 1)

        for cp in copies(g):
            cp.wait()
        km = km_ref[...]
        for n in range(bpc):
            tot = buf[slot, n * ppb]
            for p in range(1, ppb):
                tot = tot + buf[slot, n * ppb + p]
            mean = jnp.sum(tot, axis=-1, keepdims=True) * (1.0 / MOBA_BLOCK)
            km = jnp.where(lane == c * bpc + n, mean, km)
        km_ref[...] = km
        return carry

    lax.fori_loop(0, nchunks, chunk, 0)
    gate = jnp.sum(km_ref[...] * q_ref[0], axis=1, keepdims=True)
    lane_f = lane.astype(F32)
    gate = jnp.where(lane < nbp, gate, BELOW_NEG_INF)
    out = jnp.zeros_like(gate)
    for t in range(MOBA_TOPK):
        mx = jnp.max(gate, axis=-1, keepdims=True)
        idx = jnp.min(jnp.where(gate == mx, lane_f, 1e9), axis=-1, keepdims=True)
        out = jnp.where(lane == t, idx, out)
        gate = jnp.where(lane_f == idx, BELOW_NEG_INF, gate)
    sel_ref[0] = out.astype(jnp.int32)


def _moba_select(page_table, q, kcache_t, layer, nbp):
    b, npages = page_table.shape
    page = kcache_t.shape[-1]
    pc = min(npages, 8)
    assert nbp <= LANES
    grid_spec = pltpu.PrefetchScalarGridSpec(
        num_scalar_prefetch=1,
        grid=(b,),
        in_specs=[pl.BlockSpec((1, H_A, HEAD_DIM, 1), lambda i, pt: (i, 0, 0, 0)),
                  pl.BlockSpec(memory_space=pl.ANY)],
        out_specs=pl.BlockSpec((1, H_A, 1, LANES), lambda i, pt: (i, 0, 0, 0)),
        scratch_shapes=[pltpu.VMEM((DMA_SLOTS, pc, H_A, HEAD_DIM, page), F32),
                        pltpu.SemaphoreType.DMA((DMA_SLOTS,)),
                        pltpu.VMEM((H_A, HEAD_DIM, LANES), F32)],
    )
    sel = pl.pallas_call(
        functools.partial(_moba_sel_kernel, layer=layer, pc=pc, nbp=nbp),
        grid_spec=grid_spec,
        out_shape=jax.ShapeDtypeStruct((b, H_A, 1, LANES), jnp.int32),
        compiler_params=_cparams(("arbitrary",)),
        name="moba_select",
    )(page_table, q.reshape(b, H_A, HEAD_DIM, 1), kcache_t)
    return sel[:, :, 0, :MOBA_TOPK]


def _moba_dec_kernel(pt_ref, sel_ref, q_ref, kn_ref, vn_ref, bl_ref, bs_ref, kc_ref, vc_ref, o_ref,
                     kbuf, vbuf, sem, *, layer, nbp, ppb):
    b = pl.program_id(0)
    nseq = pl.num_programs(0)
    scale = HEAD_DIM ** -0.5
    npg = MOBA_TOPK * ppb

    def copies(bb, slot):
        out = []
        for h in range(H_A):
            for t in range(MOBA_TOPK):
                n = sel_ref[bb, h * MOBA_TOPK + t]
                for p in range(ppb):
                    pg = pt_ref[bb, n * ppb + p]
                    out.append(pltpu.make_async_copy(kc_ref.at[layer, pg, h], kbuf.at[slot, h, t * ppb + p],
                                                     sem.at[0, slot]))
                    out.append(pltpu.make_async_copy(vc_ref.at[layer, pg, h], vbuf.at[slot, h, t * ppb + p],
                                                     sem.at[1, slot]))
        return out

    @pl.when(b == 0)
    def _():
        for cp in copies(0, 0):
            cp.start()

    slot = b & 1

    @pl.when(b + 1 < nseq)
    def _():
        for cp in copies(b + 1, 1 - slot):
            cp.start()

    for cp in copies(b, slot):
        cp.wait()

    for h in range(H_A):
        qh = q_ref[0, h] * scale
        far = bs_ref[h, 0]
        rows = []
        for t in range(MOBA_TOPK):
            is_last = sel_ref[b, h * MOBA_TOPK + t] == nbp - 1
            for p in range(ppb):
                s = jnp.sum(kbuf[slot, h, t * ppb + p] * qh, axis=0, keepdims=True)
                rows.append(s + jnp.where(is_last, bl_ref[h, p:p + 1, :], far))
        s_sel = jnp.concatenate(rows, axis=0)
        s_new = jnp.sum(kn_ref[0, h] * qh, axis=0, keepdims=True) + bs_ref[h, 1]
        mx = jnp.maximum(jnp.max(jnp.max(s_sel, axis=1, keepdims=True), axis=0, keepdims=True), s_new)
        p_sel = jnp.exp(s_sel - mx)
        p_new = jnp.exp(s_new - mx)
        denom = jnp.sum(jnp.sum(p_sel, axis=1, keepdims=True), axis=0, keepdims=True) + p_new
        pv = vbuf[slot, h, 0] * p_sel[0:1, :]
        for g in range(1, npg):
            pv = pv + vbuf[slot, h, g] * p_sel[g:g + 1, :]
        o = jnp.sum(pv, axis=1, keepdims=True) + p_new * vn_ref[0, h]
        o_ref[0, h] = o / denom


def _t5_bucket(dist):
    n = jnp.maximum(dist, 0)
    max_exact = T5_BUCKETS // 2
    nf = jnp.maximum(n, 1).astype(F32)
    large = max_exact + (jnp.log(nf / max_exact) / math.log(T5_MAX_DIST / max_exact)
                         * (T5_BUCKETS - max_exact)).astype(jnp.int32)
    large = jnp.minimum(large, T5_BUCKETS - 1)
    return jnp.where(n < max_exact, n, large)


def _moba_decode(page_table, sel, q, k_new, v_new, kcache_t, vcache_t, layer, rel_bias, past_len):
    b, npages = page_table.shape
    page = kcache_t.shape[-1]
    ppb = MOBA_BLOCK // page
    nbp = past_len // MOBA_BLOCK
    off = jnp.arange(MOBA_BLOCK)
    dist_last = past_len - ((nbp - 1) * MOBA_BLOCK + off)
    b_last = _bias_lookup(rel_bias, _t5_bucket(dist_last)).reshape(H_A, ppb, page)
    dist_far = jnp.asarray([past_len - (nbp - 1) * MOBA_BLOCK + 1, 0])
    b_sc = _bias_lookup(rel_bias, _t5_bucket(dist_far))
    sel_flat = sel.reshape(b, H_A * MOBA_TOPK)
    col = lambda x: x.reshape(b, H_A, HEAD_DIM, 1)
    cspec = pl.BlockSpec((1, H_A, HEAD_DIM, 1), lambda i, pt, sl: (i, 0, 0, 0))
    grid_spec = pltpu.PrefetchScalarGridSpec(
        num_scalar_prefetch=2,
        grid=(b,),
        in_specs=[cspec, cspec, cspec,
                  pl.BlockSpec(b_last.shape, lambda i, pt, sl: (0, 0, 0)),
                  pl.BlockSpec(memory_space=pltpu.SMEM),
                  pl.BlockSpec(memory_space=pl.ANY),
                  pl.BlockSpec(memory_space=pl.ANY)],
        out_specs=cspec,
        scratch_shapes=[pltpu.VMEM((2, H_A, MOBA_TOPK * ppb, HEAD_DIM, page), F32),
                        pltpu.VMEM((2, H_A, MOBA_TOPK * ppb, HEAD_DIM, page), F32),
                        pltpu.SemaphoreType.DMA((2, 2))],
    )
    o = pl.pallas_call(
        functools.partial(_moba_dec_kernel, layer=layer, nbp=nbp, ppb=ppb),
        grid_spec=grid_spec,
        out_shape=jax.ShapeDtypeStruct((b, H_A, HEAD_DIM, 1), F32),
        compiler_params=_cparams(("arbitrary",)),
        name="moba_decode",
    )(page_table, sel_flat, col(q), col(k_new), col(v_new), b_last, b_sc, kcache_t, vcache_t)
    return o.reshape(b, A_W)


def _mla_dec_kernel(pt_ref, q_ref, kn_ref, mc_ref, o_ref, buf, sem, *, layer, pc):
    b = pl.program_id(0)
    nseq = pl.num_programs(0)
    npages = pt_ref.shape[1]
    nchunks = npages // pc
    q = q_ref[0]
    q_lat = q[:, 0:KV_LORA]
    q_rope = q[:, KV_LORA:KV_LORA + D_ROPE]

    nslots = buf.shape[0]
    total = nseq * nchunks

    def copies(g):
        bb = g // nchunks
        c = g % nchunks
        slot = g % nslots
        return [pltpu.make_async_copy(mc_ref.at[layer, pt_ref[bb, c * pc + p]], buf.at[slot, p], sem.at[slot])
                for p in range(pc)]

    def start(g):
        for p, cp in enumerate(copies(g)):
            cp.start(priority=p % 2)

    @pl.when(b == 0)
    def _():
        for g in range(nslots - 1):
            start(g)

    def chunk(c, carry):
        m_i, l_i, acc = carry
        g = b * nchunks + c
        slot = g % nslots

        @pl.when(g + nslots - 1 < total)
        def _():
            start(g + nslots - 1)

        for cp in copies(g):
            cp.wait()
        kv = jnp.concatenate([buf[slot, p] for p in range(pc)], axis=1).astype(BF16)
        lat = kv[0:KV_LORA]
        s = (jnp.dot(q_lat, lat, preferred_element_type=F32)
             + jnp.dot(q_rope, kv[KV_LORA:], preferred_element_type=F32))
        m_new = jnp.maximum(m_i, jnp.max(s, axis=-1, keepdims=True))
        alpha = jnp.exp2(m_i - m_new)
        p = jnp.exp2(s - m_new)
        l_new = alpha * l_i + jnp.sum(p, axis=-1, keepdims=True)
        acc_new = alpha * acc + lax.dot_general(p.astype(BF16), lat, _NT, preferred_element_type=F32)
        return m_new, l_new, acc_new

    init = (jnp.full((8, 1), NEG_INF, F32), jnp.zeros((8, 1), F32), jnp.zeros((8, KV_LORA), F32))
    m_i, l_i, acc = lax.fori_loop(0, nchunks, chunk, init)
    kn = kn_ref[0].astype(F32)
    s_new = jnp.sum(q.astype(F32) * kn, axis=-1, keepdims=True)
    m_new = jnp.maximum(m_i, s_new)
    alpha = jnp.exp2(m_i - m_new)
    p_new = jnp.exp2(s_new - m_new)
    l_fin = alpha * l_i + p_new
    acc_fin = alpha * acc + p_new * kn[:, 0:KV_LORA]
    o_ref[0] = acc_fin / l_fin


def _mla_decode(page_table, qcat, kcat_new, mcache_t, layer):
    b, npages = page_table.shape
    page = mcache_t.shape[-1]
    pc = min(npages, 16)
    q8 = jnp.pad(qcat.reshape(b, H_C, QCAT_W), ((0, 0), (0, 8 - H_C), (0, 0)))
    grid_spec = pltpu.PrefetchScalarGridSpec(
        num_scalar_prefetch=1,
        grid=(b,),
        in_specs=[pl.BlockSpec((1, 8, QCAT_W), lambda i, pt: (i, 0, 0)),
                  pl.BlockSpec((1, 1, QCAT_W), lambda i, pt: (i, 0, 0)),
                  pl.BlockSpec(memory_space=pl.ANY)],
        out_specs=pl.BlockSpec((1, 8, KV_LORA), lambda i, pt: (i, 0, 0)),
        scratch_shapes=[pltpu.VMEM((DMA_SLOTS, pc, KV_LORA + D_ROPE, page), F32),
                        pltpu.SemaphoreType.DMA((DMA_SLOTS,))],
    )
    o_lat = pl.pallas_call(
        functools.partial(_mla_dec_kernel, layer=layer, pc=pc),
        grid_spec=grid_spec,
        out_shape=jax.ShapeDtypeStruct((b, 8, KV_LORA), F32),
        compiler_params=_cparams(("arbitrary",)),
        name="mla_decode",
    )(page_table, q8, kcat_new.reshape(b, 1, QCAT_W), mcache_t)
    return o_lat[:, :H_C, :].reshape(b, H_C * KV_LORA)


def _matmul_kernel(a_ref, w_ref, o_ref):
    o_ref[...] = jnp.dot(a_ref[...].astype(BF16), w_ref[...], preferred_element_type=F32)


def _matmul(a, w):
    full = lambda x: pl.BlockSpec(x.shape, lambda: (0,) * x.ndim)
    return pl.pallas_call(
        _matmul_kernel,
        in_specs=[full(a), full(w)],
        out_specs=pl.BlockSpec((a.shape[0], w.shape[1]), lambda: (0, 0)),
        out_shape=jax.ShapeDtypeStruct((a.shape[0], w.shape[1]), F32),
        name="latent_out",
    )(a, w)


def _rope_tables(pos, base, group, width):
    half = group // 2
    inv = base ** (-jnp.arange(half, dtype=F32) / half)
    ang = pos.astype(F32)[:, None] * inv[None, :]
    cos = jnp.concatenate([jnp.cos(ang), jnp.cos(ang)], axis=1)
    sin = jnp.concatenate([-jnp.sin(ang), jnp.sin(ang)], axis=1)
    reps = width // group
    return jnp.tile(cos, (1, reps)), jnp.tile(sin, (1, reps))


def _bias_lookup(rel_bias, buckets):
    onehot = (buckets[..., None] == jnp.arange(T5_BUCKETS)).astype(F32)
    out = jnp.einsum("...k,kh->...h", onehot, rel_bias.astype(F32), precision=lax.Precision.HIGHEST)
    return jnp.moveaxis(out, -1, 0)


def _moba_bias_tiles(rel_bias):
    r = jnp.arange(MOBA_BLOCK)
    d0 = r[:, None] - r[None, :]
    own = jnp.where((d0 >= 0)[None], _bias_lookup(rel_bias, _t5_bucket(d0)), NEG_INF)
    prev = _bias_lookup(rel_bias, _t5_bucket(d0 + MOBA_BLOCK))
    far = jnp.broadcast_to(_bias_lookup(rel_bias, _t5_bucket(jnp.asarray([MOBA_BLOCK + 1])))[:, :, None], own.shape)
    return jnp.stack([own, prev, far], axis=1) * LOG2E


def _layer_weights(w_in, w_uq, w_uk, w_uv, w_o, w_up, conv_w, conv_b, w_down):
    d_ff = w_down.shape[0]
    nch = d_ff // FFN_CH
    w = {}
    w["in"] = jnp.pad(w_in, ((0, 0), (0, N_IN_PAD - N_IN))).astype(BF16)
    uq = w_uq.reshape(Q_LORA, H_C, D_NOPE + D_ROPE)
    w["uq_nope"] = uq[:, :, :D_NOPE].reshape(Q_LORA, H_C * D_NOPE).astype(BF16)
    w["uq_rope"] = jnp.pad(uq[:, :, D_NOPE:].reshape(Q_LORA, H_C * D_ROPE),
                           ((0, 0), (0, 2 * LANES - H_C * D_ROPE))).astype(BF16)
    eye = jnp.eye(H_C, dtype=F32)
    w["uk_bd"] = jnp.einsum("rhd,hg->hdgr", w_uk, eye).reshape(H_C * D_NOPE, H_C * KV_LORA).astype(BF16)
    w["uv_exp"] = jnp.einsum("rhd,hg->hrgd", w_uv, eye).reshape(H_C, KV_LORA, H_C * HEAD_DIM).astype(BF16)
    w["o_a"] = w_o[0:A_W].astype(BF16)
    w["o_b"] = w_o[A_W:A_W + R_W].astype(BF16)
    w["o_c"] = w_o[A_W + R_W:].astype(BF16)
    w["up"] = w_up.reshape(D_MODEL, 2, nch, FFN_CH).transpose(1, 2, 0, 3).astype(BF16)
    w["conv_w"] = conv_w.reshape(CONV_W, 2, nch, FFN_CH).transpose(1, 2, 0, 3)
    w["conv_b"] = conv_b.reshape(2, nch, 1, FFN_CH)
    w["down"] = w_down.reshape(nch, FFN_CH, D_MODEL).astype(BF16)
    return w


def _state_to_chunks(state, nch):
    b = state.shape[0]
    return state.reshape(b, CONV_W - 1, 2, nch, FFN_CH).transpose(0, 2, 3, 1, 4)


def _chunks_to_state(c5):
    b, _, nch, rows, ch = c5.shape
    return c5.transpose(0, 3, 1, 2, 4).reshape(b, rows, 2 * nch * ch)


def _ret_state_blocks(sfull):
    b = sfull.shape[0]
    s6 = sfull.reshape(b, H_B, HEAD_DIM, H_B, HEAD_DIM)
    return jnp.stack([s6[:, h, :, h, :] for h in range(H_B)], axis=1)


def _prompt_layer(x, w, g1, gq, gkv, gret, g2, fg, btab, tabs, final):
    b, s, _ = x.shape
    m = b * s
    x2 = x.reshape(m, D_MODEL)
    a_q, a_k, a_v, r, mm = _in_proj(x2, g1, w["in"])
    mla_new, kcat, qcat = _mla_prep(mm, gq, gkv, w["uq_nope"], w["uq_rope"], w["uk_bd"],
                                    tabs["mla_cos"], tabs["mla_sin"], s)
    o_a = _moba_attn(a_q.reshape(b, s, A_W), a_k.reshape(b, s, A_W), a_v.reshape(b, s, A_W), btab)
    o_b, sfull = _retention(r.reshape(b, s, 4 * R_W), tabs["ret_cos"], tabs["ret_sin"], gret)
    o_c = _mla_attn(qcat.reshape(b, s, H_C * QCAT_W), kcat.reshape(b, s, QCAT_W), w["uv_exp"])
    x1 = _out_proj(x2, o_a.reshape(m, A_W), o_b.reshape(m, R_W), o_c.reshape(m, A_W), w["o_a"], w["o_b"], w["o_c"])
    nch = w["down"].shape[0]
    st0 = jnp.zeros((b, 2, nch, CONV_W - 1, FFN_CH), F32)
    y, cn = _ffn(x1.reshape(b, s, D_MODEL), st0, g2, w["up"], w["conv_w"], w["conv_b"], w["down"], fg, final)
    return (y, a_k.reshape(b, s, H_A, HEAD_DIM), a_v.reshape(b, s, H_A, HEAD_DIM),
            mla_new.reshape(b, s, KV_LORA + D_ROPE), _ret_state_blocks(sfull), _chunks_to_state(cn))


def _sample_layer(x, w, g1, gq, gkv, gret, g2, fg, rel_bias, tabs, final,
                  kcache_t, vcache_t, mcache_t, layer, ret_state, conv_state, page_table, past_len):
    b = x.shape[0]
    a_q, a_k, a_v, r, mm = _in_proj(x, g1, w["in"])
    mla_new, kcat, qcat = _mla_prep(mm, gq, gkv, w["uq_nope"], w["uq_rope"], w["uk_bd"],
                                    tabs["mla_cos_s"], tabs["mla_sin_s"], 1)
    nbp = past_len // MOBA_BLOCK
    sel = _moba_select(page_table, a_q, kcache_t, layer, nbp)
    o_a = _moba_decode(page_table, sel, a_q, a_k, a_v, kcache_t, vcache_t, layer, rel_bias, past_len)
    o_b, ret_new = _retention_step(r, ret_state, past_len, gret)
    o_lat = _mla_decode(page_table, qcat, kcat, mcache_t, layer)
    o_c = _matmul(o_lat, w["uv_exp"].reshape(H_C * KV_LORA, A_W))
    x1 = _out_proj(x, o_a, o_b, o_c, w["o_a"], w["o_b"], w["o_c"])
    nch = w["down"].shape[0]
    st = _state_to_chunks(conv_state, nch).transpose(2, 1, 3, 0, 4)
    y, cn = _ffn_step(x1, st, g2, w["up"], w["conv_w"], w["conv_b"], w["down"], fg, final)
    conv_new = _chunks_to_state(cn.transpose(3, 1, 0, 2, 4))
    return (y, a_k.reshape(b, 1, H_A, HEAD_DIM), a_v.reshape(b, 1, H_A, HEAD_DIM),
            mla_new.reshape(b, 1, KV_LORA + D_ROPE), ret_new, conv_new)


def kernel(x_prompt, x_sample, cache_moba_k, cache_moba_v, cache_mla, state_ret, state_conv, page_table, norm1_g, w_in, rel_bias, ret_norm_g, mla_q_norm_g, mla_kv_norm_g, w_uq, w_uk, w_uv, w_o, norm2_g, w_up, conv_w, conv_b, w_down, final_norm_g):
    depth = w_in.shape[0]
    bp, s, _ = x_prompt.shape
    bs, s_dec, _ = x_sample.shape
    n_pool, page = cache_moba_k.shape[1], cache_moba_k.shape[2]
    past_len = page_table.shape[1] * page
    assert s % MOBA_BLOCK == 0 and s_dec == 1 and MOBA_BLOCK % page == 0
    assert past_len % MOBA_BLOCK == 0 and past_len // MOBA_BLOCK >= MOBA_TOPK
    assert w_down.shape[1] % FFN_CH == 0

    pos_p = jnp.arange(s)
    pos_s = jnp.full((bs,), past_len)
    tabs = {}
    tabs["ret_cos"], tabs["ret_sin"] = _rope_tables(pos_p, RET_ROPE_BASE, HEAD_DIM, R_W)
    tabs["mla_cos"], tabs["mla_sin"] = _rope_tables(pos_p, MLA_ROPE_BASE, D_ROPE, LANES)
    tabs["mla_cos_s"], tabs["mla_sin_s"] = _rope_tables(pos_s, MLA_ROPE_BASE, D_ROPE, LANES)
    btab = _moba_bias_tiles(rel_bias)
    fg = final_norm_g.reshape(1, D_MODEL)

    kcache_t = jnp.transpose(cache_moba_k, (0, 1, 3, 4, 2))
    vcache_t = jnp.transpose(cache_moba_v, (0, 1, 3, 4, 2))
    mcache_t = jnp.transpose(cache_mla, (0, 1, 3, 2))

    xp = x_prompt
    xs = x_sample.reshape(bs, D_MODEL)
    outs_p, outs_s = [], []
    for l in range(depth):
        final = l == depth - 1
        w = _layer_weights(w_in[l], w_uq[l], w_uk[l], w_uv[l], w_o[l], w_up[l], conv_w[l], conv_b[l], w_down[l])
        norms = (norm1_g[l].reshape(1, -1), mla_q_norm_g[l].reshape(1, -1), mla_kv_norm_g[l].reshape(1, -1),
                 ret_norm_g[l].reshape(1, -1), norm2_g[l].reshape(1, -1), fg)
        xp, *rest_p = _prompt_layer(xp, w, *norms, btab, tabs, final)
        outs_p.append(rest_p)
        xs, *rest_s = _sample_layer(
            xs, w, *norms, rel_bias, tabs, final,
            kcache_t, vcache_t, mcache_t, l, state_ret[l], state_conv[l], page_table, past_len)
        outs_s.append(rest_s)
    stack = lambda outs, i: jnp.stack([o[i] for o in outs])
    return (xp, xs.reshape(bs, 1, D_MODEL),
            stack(outs_p, 0), stack(outs_p, 1), stack(outs_p, 2), stack(outs_p, 3), stack(outs_p, 4),
            stack(outs_s, 0), stack(outs_s, 1), stack(outs_s, 2), stack(outs_s, 3), stack(outs_s, 4))
```

```python
import functools
import math

import jax
import jax.numpy as jnp
from jax import lax
from jax.experimental import pallas as pl
from jax.experimental.pallas import tpu as pltpu

F32 = jnp.float32
BF16 = jnp.bfloat16

D_MODEL = 1024
HEAD_DIM = 64
H_A = 6
H_B = 4
H_C = 6
MOBA_BLOCK = 256
MOBA_TOPK = 3
T5_BUCKETS = 32
T5_MAX_DIST = 128
RET_ROPE_BASE = 10000.0
Q_LORA = 256
KV_LORA = 128
D_NOPE = 64
D_ROPE = 32
MLA_ROPE_BASE = 10000.0
CONV_W = 3
NORM_EPS = 1e-6
NEG_INF = -1e30
BELOW_NEG_INF = -3.0e38
LOG2E = 1.4426950408889634
DEN_LANE = KV_LORA + D_ROPE

LANES = 128
A_W = H_A * HEAD_DIM
R_W = H_B * HEAD_DIM
N_IN = 3 * A_W + 4 * R_W + Q_LORA + KV_LORA + D_ROPE
N_IN_PAD = -(-N_IN // LANES) * LANES
M_W = N_IN_PAD - 3 * A_W - 4 * R_W
QCAT_W = 2 * LANES
FFN_CH = 256
VMEM_LIMIT = 56 * 1024 * 1024
DMA_SLOTS = 3

_NT = (((1,), (1,)), ((), ()))
_TN = (((0,), (0,)), ((), ()))


def _cparams(sem):
    return pltpu.CompilerParams(dimension_semantics=sem, vmem_limit_bytes=VMEM_LIMIT)


def _rms(x, g):
    return x * lax.rsqrt(jnp.mean(x * x, axis=-1, keepdims=True) + NORM_EPS) * g


def _rope_slab(x, cos, sin_signed, group):
    half = group // 2
    lane = lax.broadcasted_iota(jnp.int32, (1, LANES), 1)
    fwd = pltpu.roll(x, LANES - half, axis=1)
    bwd = pltpu.roll(x, half, axis=1)
    swapped = jnp.where((lane & (group - 1)) < half, fwd, bwd)
    return x * cos + swapped * sin_signed


def _in_proj_kernel(x_ref, g_ref, w_ref, q_ref, k_ref, v_ref, r_ref, m_ref):
    h = _rms(x_ref[...], g_ref[...]).astype(BF16)
    y = jnp.dot(h, w_ref[...], preferred_element_type=F32)
    q_ref[...] = y[:, 0:A_W]
    k_ref[...] = y[:, A_W:2 * A_W]
    v_ref[...] = y[:, 2 * A_W:3 * A_W]
    r_ref[...] = y[:, 3 * A_W:3 * A_W + 4 * R_W]
    m_ref[...] = y[:, 3 * A_W + 4 * R_W:]


def _in_proj(x, g, w):
    m = x.shape[0]
    tm = min(m, 512)
    row = lambda w_: pl.BlockSpec((tm, w_), lambda i: (i, 0))
    full = lambda a: pl.BlockSpec(a.shape, lambda i: (0,) * a.ndim)
    return pl.pallas_call(
        _in_proj_kernel,
        grid=(m // tm,),
        in_specs=[row(D_MODEL), full(g), full(w)],
        out_specs=[row(A_W), row(A_W), row(A_W), row(4 * R_W), row(M_W)],
        out_shape=[jax.ShapeDtypeStruct((m, w_), F32) for w_ in (A_W, A_W, A_W, 4 * R_W, M_W)],
        compiler_params=_cparams(("parallel",)),
        name="in_proj",
    )(x, g, w)


def _mla_prep_kernel(m_ref, gq_ref, gkv_ref, wn_ref, wr_ref, wuk_ref, cos_ref, sin_ref,
                     mla_ref, kcat_ref, qcat_ref):
    scale = (D_NOPE + D_ROPE) ** -0.5 * LOG2E
    m = m_ref[...]
    cos = cos_ref[...]
    sin = sin_ref[...]
    lane = lax.broadcasted_iota(jnp.int32, (1, LANES), 1)
    ckv = _rms(m[:, Q_LORA:Q_LORA + KV_LORA], gkv_ref[...])
    kr = _rope_slab(m[:, Q_LORA + KV_LORA:], cos, sin, D_ROPE)
    mla_ref[:, 0:KV_LORA] = ckv
    mla_ref[:, KV_LORA:] = kr[:, 0:D_ROPE]
    kcat_ref[:, 0:LANES] = ckv.astype(BF16)
    kcat_ref[:, LANES:] = jnp.where(lane < D_ROPE, kr, jnp.where(lane == D_ROPE, 1.0, 0.0)).astype(BF16)
    cq = _rms(m[:, 0:Q_LORA], gq_ref[...]).astype(BF16)
    q_nope = jnp.dot(cq, wn_ref[...], preferred_element_type=F32)
    q_rope = jnp.dot(cq, wr_ref[...], preferred_element_type=F32)
    q_lat = jnp.dot(q_nope.astype(BF16), wuk_ref[...], preferred_element_type=F32)
    per_slab = LANES // D_ROPE
    for h in range(H_C):
        s = h // per_slab
        rot = _rope_slab(q_rope[:, s * LANES:(s + 1) * LANES], cos, sin, D_ROPE)
        sh = (h % per_slab) * D_ROPE
        piece = rot if sh == 0 else pltpu.roll(rot, LANES - sh, axis=1)
        piece = jnp.where(lane < D_ROPE, piece, 0.0) * scale
        qcat_ref[:, h * QCAT_W:h * QCAT_W + LANES] = (q_lat[:, h * LANES:(h + 1) * LANES] * scale).astype(BF16)
        qcat_ref[:, h * QCAT_W + LANES:(h + 1) * QCAT_W] = piece.astype(BF16)


def _mla_prep(m_in, gq, gkv, wn, wr, wuk, cos, sin, rows_per_seq):
    m = m_in.shape[0]
    tm = min(m, 512)
    row = lambda w_: pl.BlockSpec((tm, w_), lambda i: (i, 0))
    full = lambda a: pl.BlockSpec(a.shape, lambda i: (0,) * a.ndim)
    if cos.shape[0] == rows_per_seq and rows_per_seq >= tm:
        nt = rows_per_seq // tm
        tab = pl.BlockSpec((tm, LANES), lambda i: (i % nt, 0))
    else:
        tab = pl.BlockSpec((tm, LANES), lambda i: (0, 0))
    return pl.pallas_call(
        _mla_prep_kernel,
        grid=(m // tm,),
        in_specs=[row(M_W), full(gq), full(gkv), full(wn), full(wr), full(wuk), tab, tab],
        out_specs=[row(KV_LORA + D_ROPE), row(QCAT_W), row(H_C * QCAT_W)],
        out_shape=[jax.ShapeDtypeStruct((m, KV_LORA + D_ROPE), F32),
                   jax.ShapeDtypeStruct((m, QCAT_W), BF16),
                   jax.ShapeDtypeStruct((m, H_C * QCAT_W), BF16)],
        compiler_params=_cparams(("parallel",)),
        name="mla_prep",
    )(m_in, gq, gkv, wn, wr, wuk, cos, sin)


def _mla_attn_kernel(q_ref, k_ref, wuv_ref, o_ref, m_ref, acc_ref, p_ref, s_ref, *, tq, sub):
    qi = pl.program_id(1)
    tiles = [(h, r) for h in range(H_C) for r in range(tq // sub)]

    def keys(j):
        return k_ref[0, pl.ds(pl.multiple_of(j * tq, tq), tq), :]

    def rows(h, r):
        return slice(h * tq + r * sub, h * tq + (r + 1) * sub)

    def scores(h, r, kb):
        return lax.dot_general(q_ref[0, r * sub:(r + 1) * sub, h * QCAT_W:(h + 1) * QCAT_W], kb, _NT,
                               preferred_element_type=F32)

    def visit(t):
        return jnp.where(t == 0, qi, t - 1)

    k_diag = keys(qi)
    col = lax.broadcasted_iota(jnp.int32, (sub, tq), 1)
    row = lax.broadcasted_iota(jnp.int32, (sub, tq), 0)
    for h, r in tiles:
        rs = rows(h, r)
        s_ref[rs] = jnp.where(col <= row + r * sub, scores(h, r, k_diag), NEG_INF)
        m_ref[rs] = jnp.full((sub, LANES), NEG_INF, F32)
        acc_ref[rs] = jnp.zeros((sub, QCAT_W), F32)
        p_ref[rs] = jnp.zeros((sub, tq), BF16)

    def step(t, carry):
        k_next = keys(visit(jnp.minimum(t + 1, qi)))
        k_prev = keys(visit(jnp.maximum(t - 1, 0)))
        for h, r in tiles:
            rs = rows(h, r)
            s_next = scores(h, r, k_next)
            pv = jnp.dot(p_ref[rs], k_prev, preferred_element_type=F32)
            s_cur = s_ref[rs]
            m_i = m_ref[rs]
            m_new = jnp.maximum(m_i, jnp.max(s_cur, axis=-1, keepdims=True))
            alpha = jnp.exp2(m_i - m_new)
            p_ref[rs] = jnp.exp2(s_cur - jnp.concatenate([m_new] * (tq // LANES), axis=1)).astype(BF16)
            acc_ref[rs] = jnp.concatenate([alpha] * (QCAT_W // LANES), axis=1) * (acc_ref[rs] + pv)
            s_ref[rs] = s_next
            m_ref[rs] = m_new
        return carry

    lax.fori_loop(0, qi + 1, step, 0)
    k_last = keys(visit(qi))
    out = jnp.zeros((tq, A_W), F32)
    for h in range(H_C):
        rs = slice(h * tq, (h + 1) * tq)
        acc = acc_ref[rs] + jnp.dot(p_ref[rs], k_last, preferred_element_type=F32)
        o_lat = (acc[:, 0:KV_LORA] / acc[:, DEN_LANE:DEN_LANE + 1]).astype(BF16)
        out = out + jnp.dot(o_lat, wuv_ref[h], preferred_element_type=F32)
    o_ref[0] = out


def _mla_attn(qcat, kcat, wuv_exp):
    b, s, _ = qcat.shape
    tq = min(s, 256)
    sub = min(tq, 256)
    rows = H_C * tq
    return pl.pallas_call(
        functools.partial(_mla_attn_kernel, tq=tq, sub=sub),
        grid=(b, s // tq),
        in_specs=[pl.BlockSpec((1, tq, H_C * QCAT_W), lambda i, j: (i, j, 0)),
                  pl.BlockSpec((1, s, QCAT_W), lambda i, j: (i, 0, 0)),
                  pl.BlockSpec(wuv_exp.shape, lambda i, j: (0, 0, 0))],
        out_specs=pl.BlockSpec((1, tq, A_W), lambda i, j: (i, j, 0)),
        out_shape=jax.ShapeDtypeStruct((b, s, A_W), F32),
        scratch_shapes=[pltpu.VMEM((rows, LANES), F32), pltpu.VMEM((rows, QCAT_W), F32),
                        pltpu.VMEM((rows, tq), BF16), pltpu.VMEM((rows, tq), F32)],
        compiler_params=_cparams(("parallel", "arbitrary")),
        name="mla_attn",
    )(qcat, kcat, wuv_exp)


def _moba_attn_kernel(q_ref, k_ref, v_ref, bt_ref, o_ref, km_ref, kb_ref, vb_ref, qa_ref,
                      m_ref, acc_ref, p_ref, s_ref, *, nb, sub):
    blk = MOBA_BLOCK
    qi = pl.program_id(2)
    scale = HEAD_DIM ** -0.5 * LOG2E
    lane = lax.broadcasted_iota(jnp.int32, (1, LANES), 1)
    lane_f = lane.astype(F32)

    @pl.when(qi == 0)
    def _():
        k = k_ref[0]
        v = v_ref[0]
        km_ref[...] = jnp.zeros_like(km_ref)
        for n in range(nb):
            km_ref[n:n + 1, :] = jnp.mean(k[n * blk:(n + 1) * blk, :], axis=0, keepdims=True)
        row_blk = lax.broadcasted_iota(jnp.int32, k.shape, 0) // blk
        for hh in range(2):
            hmask = (lane // HEAD_DIM) == hh
            other = lane - (1 - hh) * HEAD_DIM
            kb_ref[hh] = jnp.where(hmask, k, jnp.where(other == row_blk, 1.0, 0.0)).astype(BF16)
            vb_ref[hh] = jnp.where(hmask, v, 1.0).astype(BF16)

    q = q_ref[0]
    km = km_ref[...]
    for hh in range(2):
        hmask = (lane // HEAD_DIM) == hh
        qh = jnp.where(hmask, q, 0.0)
        gate = lax.dot_general(qh, km, _NT, precision=lax.Precision.HIGHEST, preferred_element_type=F32)
        gate = jnp.where(lane < qi, gate, NEG_INF)
        picked = jnp.zeros_like(gate)
        for _ in range(MOBA_TOPK):
            mx = jnp.max(gate, axis=-1, keepdims=True)
            idx = jnp.min(jnp.where(gate == mx, lane_f, 1e9), axis=-1, keepdims=True)
            hit = lane_f == idx
            picked = jnp.where(hit, 1.0, picked)
            gate = jnp.where(hit, BELOW_NEG_INF, gate)
        attend = ((picked > 0.0) & (lane < qi)) | (lane == qi)
        pen = jnp.where(attend, 0.0, NEG_INF)
        if hh == 0:
            pen = pltpu.roll(pen, HEAD_DIM, axis=1)
        qa_ref[hh] = jnp.where(hmask, qh * scale, pen).astype(BF16)

    tiles = [(hh, r) for hh in range(2) for r in range(blk // sub)]

    def rows(hh, r):
        return slice(hh * blk + r * sub, hh * blk + (r + 1) * sub)

    def scores(hh, r, j):
        tile = jnp.minimum(qi - j, 2)
        kb = kb_ref[hh, pl.ds(pl.multiple_of(j * blk, blk), blk), :]
        s = lax.dot_general(qa_ref[hh, r * sub:(r + 1) * sub, :], kb, _NT, preferred_element_type=F32)
        return s + bt_ref[hh, tile, r * sub:(r + 1) * sub, :]

    def weighted(hh, p, j):
        return jnp.dot(p, vb_ref[hh, pl.ds(pl.multiple_of(j * blk, blk), blk), :], preferred_element_type=F32)

    def visit(t):
        return jnp.where(t == 0, qi, t - 1)

    for hh, r in tiles:
        rs = rows(hh, r)
        s_ref[rs] = scores(hh, r, qi)
        m_ref[rs] = jnp.full((sub, LANES), NEG_INF, F32)
        acc_ref[rs] = jnp.zeros((sub, LANES), F32)
        p_ref[rs] = jnp.zeros((sub, blk), BF16)

    def step(t, carry):
        j_next = visit(jnp.minimum(t + 1, qi))
        j_prev = visit(jnp.maximum(t - 1, 0))
        for hh, r in tiles:
            rs = rows(hh, r)
            s_next = scores(hh, r, j_next)
            pv = weighted(hh, p_ref[rs], j_prev)
            s_cur = s_ref[rs]
            m_i = m_ref[rs]
            m_new = jnp.maximum(m_i, jnp.max(s_cur, axis=-1, keepdims=True))
            p_ref[rs] = jnp.exp2(s_cur - jnp.concatenate([m_new] * (blk // LANES), axis=1)).astype(BF16)
            acc_ref[rs] = jnp.exp2(m_i - m_new) * (acc_ref[rs] + pv)
            s_ref[rs] = s_next
            m_ref[rs] = m_new
        return carry

    lax.fori_loop(0, qi + 1, step, 0)
    j_last = visit(qi)
    for r in range(blk // sub):
        o = []
        for hh in range(2):
            rs = rows(hh, r)
            acc = acc_ref[rs] + weighted(hh, p_ref[rs], j_last)
            o.append(acc / pltpu.roll(acc, HEAD_DIM, axis=1))
        o_ref[0, r * sub:(r + 1) * sub, :] = jnp.where(lane < HEAD_DIM, o[0], o[1])


def _moba_attn(q, k, v, btab):
    b, s, _ = q.shape
    blk = MOBA_BLOCK
    nb = s // blk
    hp = A_W // LANES
    assert nb <= HEAD_DIM
    sub = 256
    return pl.pallas_call(
        functools.partial(_moba_attn_kernel, nb=nb, sub=sub),
        grid=(b, hp, nb),
        in_specs=[pl.BlockSpec((1, blk, LANES), lambda i, h, j: (i, j, h)),
                  pl.BlockSpec((1, s, LANES), lambda i, h, j: (i, 0, h)),
                  pl.BlockSpec((1, s, LANES), lambda i, h, j: (i, 0, h)),
                  pl.BlockSpec((2, 3, blk, blk), lambda i, h, j: (h, 0, 0, 0))],
        out_specs=pl.BlockSpec((1, blk, LANES), lambda i, h, j: (i, j, h)),
        out_shape=jax.ShapeDtypeStruct((b, s, A_W), F32),
        scratch_shapes=[pltpu.VMEM((LANES, LANES), F32),
                        pltpu.VMEM((2, s, LANES), BF16),
                        pltpu.VMEM((2, s, LANES), BF16),
                        pltpu.VMEM((2, blk, LANES), BF16),
                        pltpu.VMEM((2 * blk, LANES), F32), pltpu.VMEM((2 * blk, LANES), F32),
                        pltpu.VMEM((2 * blk, blk), BF16), pltpu.VMEM((2 * blk, blk), F32)],
        compiler_params=_cparams(("parallel", "parallel", "arbitrary")),
        name="moba_attn",
    )(q, k, v, btab)


def _ret_kernel(r_ref, cos_ref, sin_ref, intra_ref, into_ref, outof_ref, carry_ref, bd_ref, gm_ref,
                gn_ref, o_ref, st_ref, s_scr):
    ci = pl.program_id(1)

    @pl.when(ci == 0)
    def _():
        s_scr[...] = jnp.zeros_like(s_scr)

    r = r_ref[0]
    cos = cos_ref[...]
    sin = sin_ref[...]

    def rope(x):
        return jnp.concatenate(
            [_rope_slab(x[:, s * LANES:(s + 1) * LANES], cos[:, s * LANES:(s + 1) * LANES],
                        sin[:, s * LANES:(s + 1) * LANES], HEAD_DIM) for s in range(R_W // LANES)], axis=1)

    q = rope(r[:, 0:R_W])
    k = rope(r[:, R_W:2 * R_W]) * (HEAD_DIM ** -0.5)
    v = r[:, 2 * R_W:3 * R_W]
    gate = r[:, 3 * R_W:]
    lane_head = lax.broadcasted_iota(jnp.int32, (1, R_W), 1) // HEAD_DIM
    kb = k.astype(BF16)
    state = s_scr[...]
    o = jnp.dot((q * into_ref[...]).astype(BF16), state.astype(BF16), preferred_element_type=F32)
    for h in range(H_B):
        qh = jnp.where(lane_head == h, q, 0.0).astype(BF16)
        a = lax.dot_general(qh, kb, _NT, preferred_element_type=F32) * intra_ref[h]
        vh = jnp.where(lane_head == h, v, 0.0).astype(BF16)
        o = o + jnp.dot(a.astype(BF16), vh, preferred_element_type=F32)
    kv = lax.dot_general((k * outof_ref[...]).astype(BF16), v.astype(BF16), _TN, preferred_element_type=F32)
    new_state = state * carry_ref[...] + kv * bd_ref[...]
    s_scr[...] = new_state
    st_ref[0] = new_state
    ms = jnp.dot(o * o, gm_ref[...], precision=lax.Precision.HIGHEST, preferred_element_type=F32)
    ob = o * lax.rsqrt(ms + NORM_EPS) * gn_ref[...]
    o_ref[0] = ob * (gate * jax.nn.sigmoid(gate))


def _retention_tables(c):
    log_g = jnp.log(1.0 - 2.0 ** (-5.0 - jnp.arange(H_B, dtype=F32)))
    i = jnp.arange(c, dtype=F32)
    diff = i[:, None] - i[None, :]
    intra = jnp.where(diff >= 0, jnp.exp(log_g[:, None, None] * jnp.maximum(diff, 0.0)), 0.0)
    lg_lane = jnp.repeat(log_g, HEAD_DIM)
    into = jnp.exp(lg_lane[None, :] * (i[:, None] + 1.0))
    outof = jnp.exp(lg_lane[None, :] * (c - 1.0 - i[:, None]))
    carry = jnp.exp(lg_lane * c)[:, None]
    head = jnp.arange(R_W) // HEAD_DIM
    bd = (head[:, None] == head[None, :]).astype(F32)
    return intra, into, outof, carry, bd, bd / HEAD_DIM


def _retention(r, cos, sin, gn):
    b, s, _ = r.shape
    c = min(s, 256)
    intra, into, outof, carry, bd, gm = _retention_tables(c)
    full = lambda a: pl.BlockSpec(a.shape, lambda i, j: (0,) * a.ndim)
    tab = pl.BlockSpec((c, R_W), lambda i, j: (j, 0))
    return pl.pallas_call(
        _ret_kernel,
        grid=(b, s // c),
        in_specs=[pl.BlockSpec((1, c, 4 * R_W), lambda i, j: (i, j, 0)), tab, tab,
                  full(intra), full(into), full(outof), full(carry), full(bd), full(gm), full(gn)],
        out_specs=[pl.BlockSpec((1, c, R_W), lambda i, j: (i, j, 0)),
                   pl.BlockSpec((1, R_W, R_W), lambda i, j: (i, 0, 0))],
        out_shape=[jax.ShapeDtypeStruct((b, s, R_W), F32), jax.ShapeDtypeStruct((b, R_W, R_W), F32)],
        scratch_shapes=[pltpu.VMEM((R_W, R_W), F32)],
        compiler_params=_cparams(("parallel", "arbitrary")),
        name="retention",
    )(r, cos, sin, intra, into, outof, carry, bd, gm, gn)


def _ret_step_kernel(q_ref, k_ref, v_ref, g_ref, st_ref, cos_ref, sin_ref, dec_ref, gn_ref, o_ref, ns_ref):
    half = HEAD_DIM // 2
    cos = cos_ref[...]
    sin = sin_ref[...]

    def rope_col(x):
        swapped = jnp.concatenate([x[:, half:, :], x[:, :half, :]], axis=1)
        return x * cos + swapped * sin

    q = rope_col(q_ref[0])
    k = rope_col(k_ref[0]) * (HEAD_DIM ** -0.5)
    v = v_ref[0]
    new_state = st_ref[0] * dec_ref[...] + k * v
    ns_ref[0] = new_state
    o = jnp.sum(q * new_state, axis=1, keepdims=True)
    ob = o * lax.rsqrt(jnp.mean(o * o, axis=-1, keepdims=True) + NORM_EPS) * gn_ref[...]
    gate = g_ref[0]
    o_ref[0] = ob * (gate * jax.nn.sigmoid(gate))


def _retention_step(r, state, pos, gn):
    b = r.shape[0]
    col = lambda x: x.reshape(b, H_B, HEAD_DIM, 1)
    rowv = lambda x: x.reshape(b, H_B, 1, HEAD_DIM)
    half = HEAD_DIM // 2
    inv = RET_ROPE_BASE ** (-jnp.arange(half, dtype=F32) / half)
    ang = jnp.asarray(pos, F32) * inv
    cos = jnp.concatenate([jnp.cos(ang), jnp.cos(ang)]).reshape(1, HEAD_DIM, 1)
    sin = jnp.concatenate([-jnp.sin(ang), jnp.sin(ang)]).reshape(1, HEAD_DIM, 1)
    log_g = jnp.log(1.0 - 2.0 ** (-5.0 - jnp.arange(H_B, dtype=F32)))
    dec = jnp.exp(log_g * 1.0).reshape(H_B, 1, 1)
    gn4 = gn.reshape(H_B, 1, HEAD_DIM)
    cspec = pl.BlockSpec((1, H_B, HEAD_DIM, 1), lambda i: (i, 0, 0, 0))
    rspec = pl.BlockSpec((1, H_B, 1, HEAD_DIM), lambda i: (i, 0, 0, 0))
    sspec = pl.BlockSpec((1, H_B, HEAD_DIM, HEAD_DIM), lambda i: (i, 0, 0, 0))
    full = lambda a: pl.BlockSpec(a.shape, lambda i: (0,) * a.ndim)
    o, ns = pl.pallas_call(
        _ret_step_kernel,
        grid=(b,),
        in_specs=[cspec, cspec, rspec, rspec, sspec, full(cos), full(sin), full(dec), full(gn4)],
        out_specs=[rspec, sspec],
        out_shape=[jax.ShapeDtypeStruct((b, H_B, 1, HEAD_DIM), F32),
                   jax.ShapeDtypeStruct((b, H_B, HEAD_DIM, HEAD_DIM), F32)],
        compiler_params=_cparams(("parallel",)),
        name="retention_step",
    )(col(r[:, 0:R_W]), col(r[:, R_W:2 * R_W]), rowv(r[:, 2 * R_W:3 * R_W]), rowv(r[:, 3 * R_W:]),
      state, cos, sin, dec, gn4)
    return o.reshape(b, R_W), ns


def _out_proj_kernel(x_ref, a_ref, b_ref, c_ref, wa_ref, wb_ref, wc_ref, y_ref):
    y = x_ref[...]
    y = y + jnp.dot(a_ref[...].astype(BF16), wa_ref[...], preferred_element_type=F32)
    y = y + jnp.dot(b_ref[...].astype(BF16), wb_ref[...], preferred_element_type=F32)
    y = y + jnp.dot(c_ref[...].astype(BF16), wc_ref[...], preferred_element_type=F32)
    y_ref[...] = y


def _out_proj(x, oa, ob, oc, wa, wb, wc):
    m = x.shape[0]
    tm = min(m, 512)
    row = lambda w_: pl.BlockSpec((tm, w_), lambda i: (i, 0))
    full = lambda a: pl.BlockSpec(a.shape, lambda i: (0,) * a.ndim)
    return pl.pallas_call(
        _out_proj_kernel,
        grid=(m // tm,),
        in_specs=[row(D_MODEL), row(A_W), row(R_W), row(A_W), full(wa), full(wb), full(wc)],
        out_specs=row(D_MODEL),
        out_shape=jax.ShapeDtypeStruct((m, D_MODEL), F32),
        compiler_params=_cparams(("parallel",)),
        name="out_proj",
    )(x, oa, ob, oc, wa, wb, wc)


_HALO = 16


def _ffn_kernel(pt_ref, xm_ref, xh_ref, st_ref, g2_ref, wup_ref, cw_ref, cb_ref, wdn_ref, fg_ref, kc_ref,
                y_ref, cn_ref, km_ref, hs_ref, u_ref, act_ref, kbuf, ksem, *, tm, nch, final, layer, sps):
    i = pl.program_id(1)
    step = pl.program_id(0) * pl.num_programs(1) + i
    nsteps = pl.num_programs(0) * pl.num_programs(1)
    nring, pps, page = kbuf.shape[0], kbuf.shape[1], kbuf.shape[-1]
    ppb = MOBA_BLOCK // page
    bps = pps // ppb
    lane3 = lax.broadcasted_iota(jnp.int32, (1, 1, LANES), 2)

    def kcopies(seq, sl):
        ring = (seq * sps + sl) % nring
        return [pltpu.make_async_copy(kc_ref.at[layer, pt_ref[seq, sl * pps + p]], kbuf.at[ring, p], ksem.at[ring])
                for p in range(pps)]

    def kstart(seq, sl):
        for p, cp in enumerate(kcopies(seq, sl)):
            cp.start(priority=p % 2)

    @pl.when(step == 0)
    def _():
        for sl in range(nring - 1):
            kstart(0, sl)

    km_ref[0] = jnp.zeros(km_ref.shape[1:], F32)

    def stream(sl):
        ahead = sl + nring - 1
        if ahead < sps:
            kstart(step, ahead)
        else:
            @pl.when(step + 1 < nsteps)
            def _():
                kstart(step + 1, ahead - sps)
        for cp in kcopies(step, sl):
            cp.wait()
        ring = (step * sps + sl) % nring
        km = km_ref[0]
        for n in range(bps):
            tot = kbuf[ring, n * ppb]
            for p in range(1, ppb):
                tot = tot + kbuf[ring, n * ppb + p]
            mean = jnp.sum(tot, axis=-1, keepdims=True) * (1.0 / MOBA_BLOCK)
            km = jnp.where(lane3 == sl * bps + n, mean, km)
        km_ref[0] = km

    g2 = g2_ref[...]
    hs_ref[0:_HALO, :] = _rms(xh_ref[0], g2).astype(BF16)
    hs_ref[_HALO:, :] = _rms(xm_ref[0], g2).astype(BF16)
    lo = _HALO - (CONV_W - 1)
    rc = min(tm, 128)
    row8 = lax.broadcasted_iota(jnp.int32, (8, FFN_CH), 0)
    from_state = (row8 >= 8 - (CONV_W - 1)) & (i == 0)

    def up(c, slot):
        for part in range(2):
            u_ref[slot, part] = jnp.dot(hs_ref[...], wup_ref[part, c], preferred_element_type=F32)
            edge = u_ref[slot, part, _HALO - 8:_HALO, :]
            u_ref[slot, part, _HALO - 8:_HALO, :] = jnp.where(from_state, st_ref[0, part, c], edge)

    def gate(c, slot):
        for r0 in range(0, tm, rc):
            convs = []
            for part in range(2):
                w = cw_ref[part, c]
                conv = cb_ref[part, c]
                for t in range(CONV_W):
                    conv = conv + w[t:t + 1, :] * u_ref[slot, part, lo + t + r0:lo + t + r0 + rc, :]
                convs.append(conv)
            act_ref[r0:r0 + rc, c * FFN_CH:(c + 1) * FFN_CH] = (
                (convs[0] * jax.nn.sigmoid(convs[0])) * convs[1]).astype(BF16)
        for part in range(2):
            cn_ref[0, part, c] = u_ref[slot, part, _HALO + tm - (CONV_W - 1):_HALO + tm, :]

    up(0, 0)
    for c in range(nch):
        if c < sps:
            stream(c)
        if c + 1 < nch:
            up(c + 1, (c + 1) & 1)
        gate(c, c & 1)
    y = xm_ref[0] + jnp.dot(act_ref[...], wdn_ref[...], preferred_element_type=F32)
    y_ref[0] = _rms(y, fg_ref[...]) if final else y


def _ffn(x, state5, g2, wup, cw, cb, wdn, fg, final, page_table, kcache_t, layer):
    b, s, _ = x.shape
    nch = wdn.shape[0]
    tm = min(s, 512)
    nt = s // tm
    hb = tm // _HALO
    nseq, npages = page_table.shape
    page = kcache_t.shape[-1]
    nbp = npages * page // MOBA_BLOCK
    sps = max(d for d in range(1, nch + 1) if nbp % d == 0)
    pps = npages // sps
    assert nseq == b * nt and nbp <= LANES and sps >= DMA_SLOTS - 1 and pps % (MOBA_BLOCK // page) == 0
    full = lambda a: pl.BlockSpec(a.shape, lambda i, j, pt: (0,) * a.ndim)
    state8 = jnp.pad(state5, ((0, 0), (0, 0), (0, 0), (8 - (CONV_W - 1), 0), (0, 0)))
    st_spec = lambda a: pl.BlockSpec((1,) + a.shape[1:], lambda i, j, pt: (i, 0, 0, 0, 0))
    wdn = wdn.reshape(nch * FFN_CH, D_MODEL)
    km_shape = (nseq, H_A, HEAD_DIM, LANES)
    grid_spec = pltpu.PrefetchScalarGridSpec(
        num_scalar_prefetch=1,
        grid=(b, nt),
        in_specs=[pl.BlockSpec((1, tm, D_MODEL), lambda i, j, pt: (i, j, 0)),
                  pl.BlockSpec((1, _HALO, D_MODEL), lambda i, j, pt: (i, jnp.maximum(j * hb - 1, 0), 0)),
                  st_spec(state8), full(g2), full(wup), full(cw), full(cb), full(wdn), full(fg),
                  pl.BlockSpec(memory_space=pl.ANY)],
        out_specs=[pl.BlockSpec((1, tm, D_MODEL), lambda i, j, pt: (i, j, 0)), st_spec(state5),
                   pl.BlockSpec((1,) + km_shape[1:], lambda i, j, pt: (i * nt + j, 0, 0, 0))],
        scratch_shapes=[pltpu.VMEM((tm + _HALO, D_MODEL), BF16),
                        pltpu.VMEM((2, 2, tm + _HALO, FFN_CH), F32),
                        pltpu.VMEM((tm, nch * FFN_CH), BF16),
                        pltpu.VMEM((DMA_SLOTS, pps, H_A, HEAD_DIM, page), F32),
                        pltpu.SemaphoreType.DMA((DMA_SLOTS,))],
    )
    return pl.pallas_call(
        functools.partial(_ffn_kernel, tm=tm, nch=nch, final=final, layer=layer, sps=sps),
        grid_spec=grid_spec,
        out_shape=[jax.ShapeDtypeStruct((b, s, D_MODEL), F32), jax.ShapeDtypeStruct(state5.shape, F32),
                   jax.ShapeDtypeStruct(km_shape, F32)],
        compiler_params=_cparams(("arbitrary", "arbitrary")),
        name="conv_ffn",
    )(page_table, x, x, state8, g2, wup, cw, cb, wdn, fg, kcache_t)


def _ffn_step_kernel(x_ref, st_ref, g2_ref, wup_ref, cw_ref, cb_ref, wdn_ref, fg_ref,
                     y_ref, cn_ref, hs_ref, *, final):
    c = pl.program_id(0)

    @pl.when(c == 0)
    def _():
        hs_ref[...] = _rms(x_ref[...], g2_ref[...]).astype(BF16)
        y_ref[...] = x_ref[...]

    convs = []
    for part in range(2):
        u = jnp.dot(hs_ref[...], wup_ref[part, 0], preferred_element_type=F32)
        w = cw_ref[part, 0]
        s0 = st_ref[0, part, 0]
        s1 = st_ref[0, part, 1]
        convs.append(cb_ref[part, 0] + w[0:1, :] * s0 + w[1:2, :] * s1 + w[2:3, :] * u)
        cn_ref[0, part, 0] = s1
        cn_ref[0, part, 1] = u
    act = (convs[0] * jax.nn.sigmoid(convs[0])) * convs[1]
    y_ref[...] += jnp.dot(act.astype(BF16), wdn_ref[0], preferred_element_type=F32)
    if final:
        @pl.when(c == pl.num_programs(0) - 1)
        def _():
            y_ref[...] = _rms(y_ref[...], fg_ref[...])


def _ffn_step(x, state5, g2, wup, cw, cb, wdn, fg, final):
    b = x.shape[0]
    nch = wdn.shape[0]
    full = lambda a: pl.BlockSpec(a.shape, lambda c: (0,) * a.ndim)
    st_spec = pl.BlockSpec((1,) + state5.shape[1:], lambda c: (c, 0, 0, 0, 0))
    return pl.pallas_call(
        functools.partial(_ffn_step_kernel, final=final),
        grid=(nch,),
        in_specs=[full(x), st_spec, full(g2),
                  pl.BlockSpec((2, 1, D_MODEL, FFN_CH), lambda c: (0, c, 0, 0)),
                  pl.BlockSpec((2, 1, CONV_W, FFN_CH), lambda c: (0, c, 0, 0)),
                  pl.BlockSpec((2, 1, 1, FFN_CH), lambda c: (0, c, 0, 0)),
                  pl.BlockSpec((1, FFN_CH, D_MODEL), lambda c: (c, 0, 0)),
                  full(fg)],
        out_specs=[full(x), st_spec],
        out_shape=[jax.ShapeDtypeStruct((b, D_MODEL), F32), jax.ShapeDtypeStruct(state5.shape, F32)],
        scratch_shapes=[pltpu.VMEM((b, D_MODEL), BF16)],
        compiler_params=_cparams(("arbitrary",)),
        name="conv_ffn_step",
    )(x, state5, g2, wup, cw, cb, wdn, fg)


def _moba_sel_kernel(q_ref, km_ref, sel_ref, *, nbp):
    lane = lax.broadcasted_iota(jnp.int32, (1, 1, LANES), 2)
    gate = jnp.sum(km_ref[0] * q_ref[0], axis=1, keepdims=True)
    lane_f = lane.astype(F32)
    gate = jnp.where(lane < nbp, gate, BELOW_NEG_INF)
    out = jnp.zeros_like(gate)
    for t in range(MOBA_TOPK):
        mx = jnp.max(gate, axis=-1, keepdims=True)
        idx = jnp.min(jnp.where(gate == mx, lane_f, 1e9), axis=-1, keepdims=True)
        out = jnp.where(lane == t, idx, out)
        gate = jnp.where(lane_f == idx, BELOW_NEG_INF, gate)
    sel_ref[0] = out.astype(jnp.int32)


def _moba_select(q, kmean, nbp):
    b = q.shape[0]
    sel = pl.pallas_call(
        functools.partial(_moba_sel_kernel, nbp=nbp),
        grid=(b,),
        in_specs=[pl.BlockSpec((1, H_A, HEAD_DIM, 1), lambda i: (i, 0, 0, 0)),
                  pl.BlockSpec((1, H_A, HEAD_DIM, LANES), lambda i: (i, 0, 0, 0))],
        out_specs=pl.BlockSpec((1, H_A, 1, LANES), lambda i: (i, 0, 0, 0)),
        out_shape=jax.ShapeDtypeStruct((b, H_A, 1, LANES), jnp.int32),
        compiler_params=_cparams(("parallel",)),
        name="moba_select",
    )(q.reshape(b, H_A, HEAD_DIM, 1), kmean)
    return sel[:, :, 0, :MOBA_TOPK]


def _moba_dec_kernel(pt_ref, sel_ref, q_ref, kn_ref, vn_ref, bl_ref, bs_ref, kc_ref, vc_ref, o_ref,
                     kbuf, vbuf, sem, *, layer, nbp, ppb):
    b = pl.program_id(0)
    nseq = pl.num_programs(0)
    scale = HEAD_DIM ** -0.5
    npg = MOBA_TOPK * ppb

    def copies(bb, slot):
        out = []
        for h in range(H_A):
            for t in range(MOBA_TOPK):
                n = sel_ref[bb, h * MOBA_TOPK + t]
                for p in range(ppb):
                    pg = pt_ref[bb, n * ppb + p]
                    out.append(pltpu.make_async_copy(kc_ref.at[layer, pg, h], kbuf.at[slot, h, t * ppb + p],
                                                     sem.at[0, slot]))
                    out.append(pltpu.make_async_copy(vc_ref.at[layer, pg, h], vbuf.at[slot, h, t * ppb + p],
                                                     sem.at[1, slot]))
        return out

    @pl.when(b == 0)
    def _():
        for cp in copies(0, 0):
            cp.start()

    slot = b & 1

    @pl.when(b + 1 < nseq)
    def _():
        for cp in copies(b + 1, 1 - slot):
            cp.start()

    for cp in copies(b, slot):
        cp.wait()

    for h in range(H_A):
        qh = q_ref[0, h] * scale
        far = bs_ref[h, 0]
        rows = []
        for t in range(MOBA_TOPK):
            is_last = sel_ref[b, h * MOBA_TOPK + t] == nbp - 1
            for p in range(ppb):
                s = jnp.sum(kbuf[slot, h, t * ppb + p] * qh, axis=0, keepdims=True)
                rows.append(s + jnp.where(is_last, bl_ref[h, p:p + 1, :], far))
        s_sel = jnp.concatenate(rows, axis=0)
        s_new = jnp.sum(kn_ref[0, h] * qh, axis=0, keepdims=True) + bs_ref[h, 1]
        mx = jnp.maximum(jnp.max(jnp.max(s_sel, axis=1, keepdims=True), axis=0, keepdims=True), s_new)
        p_sel = jnp.exp(s_sel - mx)
        p_new = jnp.exp(s_new - mx)
        denom = jnp.sum(jnp.sum(p_sel, axis=1, keepdims=True), axis=0, keepdims=True) + p_new
        pv = vbuf[slot, h, 0] * p_sel[0:1, :]
        for g in range(1, npg):
            pv = pv + vbuf[slot, h, g] * p_sel[g:g + 1, :]
        o = jnp.sum(pv, axis=1, keepdims=True) + p_new * vn_ref[0, h]
        o_ref[0, h] = o / denom


def _t5_bucket(dist):
    n = jnp.maximum(dist, 0)
    max_exact = T5_BUCKETS // 2
    nf = jnp.maximum(n, 1).astype(F32)
    large = max_exact + (jnp.log(nf / max_exact) / math.log(T5_MAX_DIST / max_exact)
                         * (T5_BUCKETS - max_exact)).astype(jnp.int32)
    large = jnp.minimum(large, T5_BUCKETS - 1)
    return jnp.where(n < max_exact, n, large)


def _moba_decode(page_table, sel, q, k_new, v_new, kcache_t, vcache_t, layer, rel_bias, past_len):
    b, npages = page_table.shape
    page = kcache_t.shape[-1]
    ppb = MOBA_BLOCK // page
    nbp = past_len // MOBA_BLOCK
    off = jnp.arange(MOBA_BLOCK)
    dist_last = past_len - ((nbp - 1) * MOBA_BLOCK + off)
    b_last = _bias_lookup(rel_bias, _t5_bucket(dist_last)).reshape(H_A, ppb, page)
    dist_far = jnp.asarray([past_len - (nbp - 1) * MOBA_BLOCK + 1, 0])
    b_sc = _bias_lookup(rel_bias, _t5_bucket(dist_far))
    sel_flat = sel.reshape(b, H_A * MOBA_TOPK)
    col = lambda x: x.reshape(b, H_A, HEAD_DIM, 1)
    cspec = pl.BlockSpec((1, H_A, HEAD_DIM, 1), lambda i, pt, sl: (i, 0, 0, 0))
    grid_spec = pltpu.PrefetchScalarGridSpec(
        num_scalar_prefetch=2,
        grid=(b,),
        in_specs=[cspec, cspec, cspec,
                  pl.BlockSpec(b_last.shape, lambda i, pt, sl: (0, 0, 0)),
                  pl.BlockSpec(memory_space=pltpu.SMEM),
                  pl.BlockSpec(memory_space=pl.ANY),
                  pl.BlockSpec(memory_space=pl.ANY)],
        out_specs=cspec,
        scratch_shapes=[pltpu.VMEM((2, H_A, MOBA_TOPK * ppb, HEAD_DIM, page), F32),
                        pltpu.VMEM((2, H_A, MOBA_TOPK * ppb, HEAD_DIM, page), F32),
                        pltpu.SemaphoreType.DMA((2, 2))],
    )
    o = pl.pallas_call(
        functools.partial(_moba_dec_kernel, layer=layer, nbp=nbp, ppb=ppb),
        grid_spec=grid_spec,
        out_shape=jax.ShapeDtypeStruct((b, H_A, HEAD_DIM, 1), F32),
        compiler_params=_cparams(("arbitrary",)),
        name="moba_decode",
    )(page_table, sel_flat, col(q), col(k_new), col(v_new), b_last, b_sc, kcache_t, vcache_t)
    return o.reshape(b, A_W)


def _mla_dec_kernel(pt_ref, q_ref, kn_ref, mc_ref, o_ref, buf, sem, *, layer, pc):
    b = pl.program_id(0)
    nseq = pl.num_programs(0)
    npages = pt_ref.shape[1]
    nchunks = npages // pc
    q = q_ref[0]
    q_lat = q[:, 0:KV_LORA]
    q_rope = q[:, KV_LORA:KV_LORA + D_ROPE]

    nslots = buf.shape[0]
    total = nseq * nchunks

    def copies(g):
        bb = g // nchunks
        c = g % nchunks
        slot = g % nslots
        return [pltpu.make_async_copy(mc_ref.at[layer, pt_ref[bb, c * pc + p]], buf.at[slot, p], sem.at[slot])
                for p in range(pc)]

    def start(g):
        for p, cp in enumerate(copies(g)):
            cp.start(priority=p % 2)

    @pl.when(b == 0)
    def _():
        for g in range(nslots - 1):
            start(g)

    def chunk(c, carry):
        m_i, l_i, acc = carry
        g = b * nchunks + c
        slot = g % nslots

        @pl.when(g + nslots - 1 < total)
        def _():
            start(g + nslots - 1)

        for cp in copies(g):
            cp.wait()
        kv = jnp.concatenate([buf[slot, p] for p in range(pc)], axis=1).astype(BF16)
        lat = kv[0:KV_LORA]
        s = (jnp.dot(q_lat, lat, preferred_element_type=F32)
             + jnp.dot(q_rope, kv[KV_LORA:], preferred_element_type=F32))
        m_new = jnp.maximum(m_i, jnp.max(s, axis=-1, keepdims=True))
        alpha = jnp.exp2(m_i - m_new)
        p = jnp.exp2(s - m_new)
        l_new = alpha * l_i + jnp.sum(p, axis=-1, keepdims=True)
        acc_new = alpha * acc + lax.dot_general(p.astype(BF16), lat, _NT, preferred_element_type=F32)
        return m_new, l_new, acc_new

    init = (jnp.full((8, 1), NEG_INF, F32), jnp.zeros((8, 1), F32), jnp.zeros((8, KV_LORA), F32))
    m_i, l_i, acc = lax.fori_loop(0, nchunks, chunk, init)
    kn = kn_ref[0].astype(F32)
    s_new = jnp.sum(q.astype(F32) * kn, axis=-1, keepdims=True)
    m_new = jnp.maximum(m_i, s_new)
    alpha = jnp.exp2(m_i - m_new)
    p_new = jnp.exp2(s_new - m_new)
    l_fin = alpha * l_i + p_new
    acc_fin = alpha * acc + p_new * kn[:, 0:KV_LORA]
    o_ref[0] = acc_fin / l_fin


def _mla_decode(page_table, qcat, kcat_new, mcache_t, layer):
    b, npages = page_table.shape
    page = mcache_t.shape[-1]
    pc = min(npages, 16)
    q8 = jnp.pad(qcat.reshape(b, H_C, QCAT_W), ((0, 0), (0, 8 - H_C), (0, 0)))
    grid_spec = pltpu.PrefetchScalarGridSpec(
        num_scalar_prefetch=1,
        grid=(b,),
        in_specs=[pl.BlockSpec((1, 8, QCAT_W), lambda i, pt: (i, 0, 0)),
                  pl.BlockSpec((1, 1, QCAT_W), lambda i, pt: (i, 0, 0)),
                  pl.BlockSpec(memory_space=pl.ANY)],
        out_specs=pl.BlockSpec((1, 8, KV_LORA), lambda i, pt: (i, 0, 0)),
        scratch_shapes=[pltpu.VMEM((DMA_SLOTS, pc, KV_LORA + D_ROPE, page), F32),
                        pltpu.SemaphoreType.DMA((DMA_SLOTS,))],
    )
    o_lat = pl.pallas_call(
        functools.partial(_mla_dec_kernel, layer=layer, pc=pc),
        grid_spec=grid_spec,
        out_shape=jax.ShapeDtypeStruct((b, 8, KV_LORA), F32),
        compiler_params=_cparams(("arbitrary",)),
        name="mla_decode",
    )(page_table, q8, kcat_new.reshape(b, 1, QCAT_W), mcache_t)
    return o_lat[:, :H_C, :].reshape(b, H_C * KV_LORA)


def _matmul_kernel(a_ref, w_ref, o_ref):
    o_ref[...] = jnp.dot(a_ref[...].astype(BF16), w_ref[...], preferred_element_type=F32)


def _matmul(a, w):
    full = lambda x: pl.BlockSpec(x.shape, lambda: (0,) * x.ndim)
    return pl.pallas_call(
        _matmul_kernel,
        in_specs=[full(a), full(w)],
        out_specs=pl.BlockSpec((a.shape[0], w.shape[1]), lambda: (0, 0)),
        out_shape=jax.ShapeDtypeStruct((a.shape[0], w.shape[1]), F32),
        name="latent_out",
    )(a, w)


def _rope_tables(pos, base, group, width):
    half = group // 2
    inv = base ** (-jnp.arange(half, dtype=F32) / half)
    ang = pos.astype(F32)[:, None] * inv[None, :]
    cos = jnp.concatenate([jnp.cos(ang), jnp.cos(ang)], axis=1)
    sin = jnp.concatenate([-jnp.sin(ang), jnp.sin(ang)], axis=1)
    reps = width // group
    return jnp.tile(cos, (1, reps)), jnp.tile(sin, (1, reps))


def _bias_lookup(rel_bias, buckets):
    onehot = (buckets[..., None] == jnp.arange(T5_BUCKETS)).astype(F32)
    out = jnp.einsum("...k,kh->...h", onehot, rel_bias.astype(F32), precision=lax.Precision.HIGHEST)
    return jnp.moveaxis(out, -1, 0)


def _moba_bias_tiles(rel_bias):
    r = jnp.arange(MOBA_BLOCK)
    d0 = r[:, None] - r[None, :]
    own = jnp.where((d0 >= 0)[None], _bias_lookup(rel_bias, _t5_bucket(d0)), NEG_INF)
    prev = _bias_lookup(rel_bias, _t5_bucket(d0 + MOBA_BLOCK))
    far = jnp.broadcast_to(_bias_lookup(rel_bias, _t5_bucket(jnp.asarray([MOBA_BLOCK + 1])))[:, :, None], own.shape)
    return jnp.stack([own, prev, far], axis=1) * LOG2E


def _layer_weights(w_in, w_uq, w_uk, w_uv, w_o, w_up, conv_w, conv_b, w_down):
    d_ff = w_down.shape[0]
    nch = d_ff // FFN_CH
    w = {}
    w["in"] = jnp.pad(w_in, ((0, 0), (0, N_IN_PAD - N_IN))).astype(BF16)
    uq = w_uq.reshape(Q_LORA, H_C, D_NOPE + D_ROPE)
    w["uq_nope"] = uq[:, :, :D_NOPE].reshape(Q_LORA, H_C * D_NOPE).astype(BF16)
    w["uq_rope"] = jnp.pad(uq[:, :, D_NOPE:].reshape(Q_LORA, H_C * D_ROPE),
                           ((0, 0), (0, 2 * LANES - H_C * D_ROPE))).astype(BF16)
    eye = jnp.eye(H_C, dtype=F32)
    w["uk_bd"] = jnp.einsum("rhd,hg->hdgr", w_uk, eye).reshape(H_C * D_NOPE, H_C * KV_LORA).astype(BF16)
    w["uv_exp"] = jnp.einsum("rhd,hg->hrgd", w_uv, eye).reshape(H_C, KV_LORA, H_C * HEAD_DIM).astype(BF16)
    w["o_a"] = w_o[0:A_W].astype(BF16)
    w["o_b"] = w_o[A_W:A_W + R_W].astype(BF16)
    w["o_c"] = w_o[A_W + R_W:].astype(BF16)
    w["up"] = w_up.reshape(D_MODEL, 2, nch, FFN_CH).transpose(1, 2, 0, 3).astype(BF16)
    w["conv_w"] = conv_w.reshape(CONV_W, 2, nch, FFN_CH).transpose(1, 2, 0, 3)
    w["conv_b"] = conv_b.reshape(2, nch, 1, FFN_CH)
    w["down"] = w_down.reshape(nch, FFN_CH, D_MODEL).astype(BF16)
    return w


def _state_to_chunks(state, nch):
    b = state.shape[0]
    return state.reshape(b, CONV_W - 1, 2, nch, FFN_CH).transpose(0, 2, 3, 1, 4)


def _chunks_to_state(c5):
    b, _, nch, rows, ch = c5.shape
    return c5.transpose(0, 3, 1, 2, 4).reshape(b, rows, 2 * nch * ch)


def _ret_state_blocks(sfull):
    b = sfull.shape[0]
    s6 = sfull.reshape(b, H_B, HEAD_DIM, H_B, HEAD_DIM)
    return jnp.stack([s6[:, h, :, h, :] for h in range(H_B)], axis=1)


def _prompt_layer(x, w, g1, gq, gkv, gret, g2, fg, btab, tabs, final, page_table, kcache_t, layer):
    b, s, _ = x.shape
    m = b * s
    x2 = x.reshape(m, D_MODEL)
    a_q, a_k, a_v, r, mm = _in_proj(x2, g1, w["in"])
    mla_new, kcat, qcat = _mla_prep(mm, gq, gkv, w["uq_nope"], w["uq_rope"], w["uk_bd"],
                                    tabs["mla_cos"], tabs["mla_sin"], s)
    o_a = _moba_attn(a_q.reshape(b, s, A_W), a_k.reshape(b, s, A_W), a_v.reshape(b, s, A_W), btab)
    o_b, sfull = _retention(r.reshape(b, s, 4 * R_W), tabs["ret_cos"], tabs["ret_sin"], gret)
    o_c = _mla_attn(qcat.reshape(b, s, H_C * QCAT_W), kcat.reshape(b, s, QCAT_W), w["uv_exp"])
    x1 = _out_proj(x2, o_a.reshape(m, A_W), o_b.reshape(m, R_W), o_c.reshape(m, A_W), w["o_a"], w["o_b"], w["o_c"])
    nch = w["down"].shape[0]
    st0 = jnp.zeros((b, 2, nch, CONV_W - 1, FFN_CH), F32)
    y, cn, kmean = _ffn(x1.reshape(b, s, D_MODEL), st0, g2, w["up"], w["conv_w"], w["conv_b"], w["down"], fg, final,
                        page_table, kcache_t, layer)
    return (y, kmean, a_k.reshape(b, s, H_A, HEAD_DIM), a_v.reshape(b, s, H_A, HEAD_DIM),
            mla_new.reshape(b, s, KV_LORA + D_ROPE), _ret_state_blocks(sfull), _chunks_to_state(cn))


def _sample_layer(x, w, g1, gq, gkv, gret, g2, fg, rel_bias, tabs, final,
                  kmean, kcache_t, vcache_t, mcache_t, layer, ret_state, conv_state, page_table, past_len):
    b = x.shape[0]
    a_q, a_k, a_v, r, mm = _in_proj(x, g1, w["in"])
    mla_new, kcat, qcat = _mla_prep(mm, gq, gkv, w["uq_nope"], w["uq_rope"], w["uk_bd"],
                                    tabs["mla_cos_s"], tabs["mla_sin_s"], 1)
    nbp = past_len // MOBA_BLOCK
    sel = _moba_select(a_q, kmean, nbp)
    o_a = _moba_decode(page_table, sel, a_q, a_k, a_v, kcache_t, vcache_t, layer, rel_bias, past_len)
    o_b, ret_new = _retention_step(r, ret_state, past_len, gret)
    o_lat = _mla_decode(page_table, qcat, kcat, mcache_t, layer)
    o_c = _matmul(o_lat, w["uv_exp"].reshape(H_C * KV_LORA, A_W))
    x1 = _out_proj(x, o_a, o_b, o_c, w["o_a"], w["o_b"], w["o_c"])
    nch = w["down"].shape[0]
    st = _state_to_chunks(conv_state, nch).transpose(2, 1, 3, 0, 4)
    y, cn = _ffn_step(x1, st, g2, w["up"], w["conv_w"], w["conv_b"], w["down"], fg, final)
    conv_new = _chunks_to_state(cn.transpose(3, 1, 0, 2, 4))
    return (y, a_k.reshape(b, 1, H_A, HEAD_DIM), a_v.reshape(b, 1, H_A, HEAD_DIM),
            mla_new.reshape(b, 1, KV_LORA + D_ROPE), ret_new, conv_new)


def kernel(x_prompt, x_sample, cache_moba_k, cache_moba_v, cache_mla, state_ret, state_conv, page_table, norm1_g, w_in, rel_bias, ret_norm_g, mla_q_norm_g, mla_kv_norm_g, w_uq, w_uk, w_uv, w_o, norm2_g, w_up, conv_w, conv_b, w_down, final_norm_g):
    depth = w_in.shape[0]
    bp, s, _ = x_prompt.shape
    bs, s_dec, _ = x_sample.shape
    n_pool, page = cache_moba_k.shape[1], cache_moba_k.shape[2]
    past_len = page_table.shape[1] * page
    assert s % MOBA_BLOCK == 0 and s_dec == 1 and MOBA_BLOCK % page == 0
    assert past_len % MOBA_BLOCK == 0 and past_len // MOBA_BLOCK >= MOBA_TOPK
    assert w_down.shape[1] % FFN_CH == 0

    pos_p = jnp.arange(s)
    pos_s = jnp.full((bs,), past_len)
    tabs = {}
    tabs["ret_cos"], tabs["ret_sin"] = _rope_tables(pos_p, RET_ROPE_BASE, HEAD_DIM, R_W)
    tabs["mla_cos"], tabs["mla_sin"] = _rope_tables(pos_p, MLA_ROPE_BASE, D_ROPE, LANES)
    tabs["mla_cos_s"], tabs["mla_sin_s"] = _rope_tables(pos_s, MLA_ROPE_BASE, D_ROPE, LANES)
    btab = _moba_bias_tiles(rel_bias)
    fg = final_norm_g.reshape(1, D_MODEL)

    kcache_t = jnp.transpose(cache_moba_k, (0, 1, 3, 4, 2))
    vcache_t = jnp.transpose(cache_moba_v, (0, 1, 3, 4, 2))
    mcache_t = jnp.transpose(cache_mla, (0, 1, 3, 2))

    xp = x_prompt
    xs = x_sample.reshape(bs, D_MODEL)
    outs_p, outs_s = [], []
    for l in range(depth):
        final = l == depth - 1
        w = _layer_weights(w_in[l], w_uq[l], w_uk[l], w_uv[l], w_o[l], w_up[l], conv_w[l], conv_b[l], w_down[l])
        norms = (norm1_g[l].reshape(1, -1), mla_q_norm_g[l].reshape(1, -1), mla_kv_norm_g[l].reshape(1, -1),
                 ret_norm_g[l].reshape(1, -1), norm2_g[l].reshape(1, -1), fg)
        xp, kmean, *rest_p = _prompt_layer(xp, w, *norms, btab, tabs, final, page_table, kcache_t, l)
        outs_p.append(rest_p)
        xs, *rest_s = _sample_layer(
            xs, w, *norms, rel_bias, tabs, final,
            kmean, kcache_t, vcache_t, mcache_t, l, state_ret[l], state_conv[l], page_table, past_len)
        outs_s.append(rest_s)
    stack = lambda outs, i: jnp.stack([o[i] for o in outs])
    return (xp, xs.reshape(bs, 1, D_MODEL),
            stack(outs_p, 0), stack(outs_p, 1), stack(outs_p, 2), stack(outs_p, 3), stack(outs_p, 4),
            stack(outs_s, 0), stack(outs_s, 1), stack(outs_s, 2), stack(outs_s, 3), stack(outs_s, 4))
```

```python
import functools
import math

import jax
import jax.numpy as jnp
from jax import lax
from jax.experimental import pallas as pl
from jax.experimental.pallas import tpu as pltpu

F32 = jnp.float32
BF16 = jnp.bfloat16

D_MODEL = 1024
HEAD_DIM = 64
H_A = 6
H_B = 4
H_C = 6
MOBA_BLOCK = 256
MOBA_TOPK = 3
T5_BUCKETS = 32
T5_MAX_DIST = 128
RET_ROPE_BASE = 10000.0
Q_LORA = 256
KV_LORA = 128
D_NOPE = 64
D_ROPE = 32
MLA_ROPE_BASE = 10000.0
CONV_W = 3
NORM_EPS = 1e-6
NEG_INF = -1e30
BELOW_NEG_INF = -3.0e38
LOG2E = 1.4426950408889634
DEN_LANE = KV_LORA + D_ROPE

LANES = 128
A_W = H_A * HEAD_DIM
R_W = H_B * HEAD_DIM
N_IN = 3 * A_W + 4 * R_W + Q_LORA + KV_LORA + D_ROPE
N_IN_PAD = -(-N_IN // LANES) * LANES
M_W = N_IN_PAD - 3 * A_W - 4 * R_W
QCAT_W = 2 * LANES
FFN_CH = 256
VMEM_LIMIT = 56 * 1024 * 1024
DMA_SLOTS = 3

_NT = (((1,), (1,)), ((), ()))
_TN = (((0,), (0,)), ((), ()))


def _cparams(sem):
    return pltpu.CompilerParams(dimension_semantics=sem, vmem_limit_bytes=VMEM_LIMIT)


def _rms(x, g):
    return x * lax.rsqrt(jnp.mean(x * x, axis=-1, keepdims=True) + NORM_EPS) * g


def _rope_slab(x, cos, sin_signed, group):
    half = group // 2
    lane = lax.broadcasted_iota(jnp.int32, (1, LANES), 1)
    fwd = pltpu.roll(x, LANES - half, axis=1)
    bwd = pltpu.roll(x, half, axis=1)
    swapped = jnp.where((lane & (group - 1)) < half, fwd, bwd)
    return x * cos + swapped * sin_signed


def _in_proj_kernel(x_ref, g_ref, w_ref, gq_ref, gkv_ref, wn_ref, wr_ref, wuk_ref, cos_ref, sin_ref,
                    q_ref, k_ref, v_ref, r_ref, mla_ref, kcat_ref, qcat_ref):
    h = _rms(x_ref[...], g_ref[...]).astype(BF16)
    y = jnp.dot(h, w_ref[...], preferred_element_type=F32)
    q_ref[...] = y[:, 0:A_W]
    k_ref[...] = y[:, A_W:2 * A_W]
    v_ref[...] = y[:, 2 * A_W:3 * A_W]
    r_ref[...] = y[:, 3 * A_W:3 * A_W + 4 * R_W]
    _mla_prep_rows(y[:, 3 * A_W + 4 * R_W:], gq_ref, gkv_ref, wn_ref, wr_ref, wuk_ref, cos_ref, sin_ref,
                   mla_ref, kcat_ref, qcat_ref)


def _in_proj(x, g, w, gq, gkv, wn, wr, wuk, cos, sin, rows_per_seq):
    m = x.shape[0]
    tm = min(m, 512)
    row = lambda w_: pl.BlockSpec((tm, w_), lambda i: (i, 0))
    full = lambda a: pl.BlockSpec(a.shape, lambda i: (0,) * a.ndim)
    if cos.shape[0] == rows_per_seq and rows_per_seq >= tm:
        nt = rows_per_seq // tm
        tab = pl.BlockSpec((tm, LANES), lambda i: (i % nt, 0))
    else:
        tab = pl.BlockSpec((tm, LANES), lambda i: (0, 0))
    widths = (A_W, A_W, A_W, 4 * R_W, KV_LORA + D_ROPE, QCAT_W, H_C * QCAT_W)
    dtypes = (F32, F32, F32, F32, F32, BF16, BF16)
    return pl.pallas_call(
        _in_proj_kernel,
        grid=(m // tm,),
        in_specs=[row(D_MODEL), full(g), full(w), full(gq), full(gkv), full(wn), full(wr), full(wuk), tab, tab],
        out_specs=[row(w_) for w_ in widths],
        out_shape=[jax.ShapeDtypeStruct((m, w_), dt) for w_, dt in zip(widths, dtypes)],
        compiler_params=_cparams(("parallel",)),
        name="in_proj",
    )(x, g, w, gq, gkv, wn, wr, wuk, cos, sin)


def _mla_prep_rows(m, gq_ref, gkv_ref, wn_ref, wr_ref, wuk_ref, cos_ref, sin_ref, mla_ref, kcat_ref, qcat_ref):
    scale = (D_NOPE + D_ROPE) ** -0.5 * LOG2E
    cos = cos_ref[...]
    sin = sin_ref[...]
    lane = lax.broadcasted_iota(jnp.int32, (1, LANES), 1)
    ckv = _rms(m[:, Q_LORA:Q_LORA + KV_LORA], gkv_ref[...])
    kr = _rope_slab(m[:, Q_LORA + KV_LORA:], cos, sin, D_ROPE)
    mla_ref[:, 0:KV_LORA] = ckv
    mla_ref[:, KV_LORA:] = kr[:, 0:D_ROPE]
    kcat_ref[:, 0:LANES] = ckv.astype(BF16)
    kcat_ref[:, LANES:] = jnp.where(lane < D_ROPE, kr, jnp.where(lane == D_ROPE, 1.0, 0.0)).astype(BF16)
    cq = _rms(m[:, 0:Q_LORA], gq_ref[...]).astype(BF16)
    q_nope = jnp.dot(cq, wn_ref[...], preferred_element_type=F32)
    q_rope = jnp.dot(cq, wr_ref[...], preferred_element_type=F32)
    q_lat = jnp.dot(q_nope.astype(BF16), wuk_ref[...], preferred_element_type=F32)
    per_slab = LANES // D_ROPE
    for h in range(H_C):
        s = h // per_slab
        rot = _rope_slab(q_rope[:, s * LANES:(s + 1) * LANES], cos, sin, D_ROPE)
        sh = (h % per_slab) * D_ROPE
        piece = rot if sh == 0 else pltpu.roll(rot, LANES - sh, axis=1)
        piece = jnp.where(lane < D_ROPE, piece, 0.0) * scale
        qcat_ref[:, h * QCAT_W:h * QCAT_W + LANES] = (q_lat[:, h * LANES:(h + 1) * LANES] * scale).astype(BF16)
        qcat_ref[:, h * QCAT_W + LANES:(h + 1) * QCAT_W] = piece.astype(BF16)


def _mla_attn_kernel(q_ref, k_ref, wuv_ref, o_ref, m_ref, acc_ref, p_ref, s_ref, *, tq, sub):
    qi = pl.program_id(1)
    tiles = [(h, r) for h in range(H_C) for r in range(tq // sub)]

    def keys(j):
        return k_ref[0, pl.ds(pl.multiple_of(j * tq, tq), tq), :]

    def rows(h, r):
        return slice(h * tq + r * sub, h * tq + (r + 1) * sub)

    def scores(h, r, kb):
        return lax.dot_general(q_ref[0, r * sub:(r + 1) * sub, h * QCAT_W:(h + 1) * QCAT_W], kb, _NT,
                               preferred_element_type=F32)

    def visit(t):
        return jnp.where(t == 0, qi, t - 1)

    k_diag = keys(qi)
    k_first = keys(visit(jnp.minimum(1, qi)))
    col = lax.broadcasted_iota(jnp.int32, (sub, tq), 1)
    row = lax.broadcasted_iota(jnp.int32, (sub, tq), 0)
    for h, r in tiles:
        rs = rows(h, r)
        s_diag = jnp.where(col <= row + r * sub, scores(h, r, k_diag), NEG_INF)
        m_diag = jnp.broadcast_to(jnp.max(s_diag, axis=-1, keepdims=True), (sub, LANES))
        p_ref[rs] = jnp.exp2(s_diag - jnp.concatenate([m_diag] * (tq // LANES), axis=1)).astype(BF16)
        m_ref[rs] = m_diag
        acc_ref[rs] = jnp.zeros((sub, QCAT_W), F32)
        s_ref[rs] = scores(h, r, k_first)

    def step(t, carry):
        k_next = keys(visit(jnp.minimum(t + 1, qi)))
        k_prev = keys(visit(jnp.maximum(t - 1, 0)))
        for h, r in tiles:
            rs = rows(h, r)
            s_next = scores(h, r, k_next)
            pv = jnp.dot(p_ref[rs], k_prev, preferred_element_type=F32)
            s_cur = s_ref[rs]
            m_i = m_ref[rs]
            m_new = jnp.maximum(m_i, jnp.max(s_cur, axis=-1, keepdims=True))
            alpha = jnp.exp2(m_i - m_new)
            p_ref[rs] = jnp.exp2(s_cur - jnp.concatenate([m_new] * (tq // LANES), axis=1)).astype(BF16)
            acc_ref[rs] = jnp.concatenate([alpha] * (QCAT_W // LANES), axis=1) * (acc_ref[rs] + pv)
            s_ref[rs] = s_next
            m_ref[rs] = m_new
        return carry

    lax.fori_loop(1, qi + 1, step, 0)
    k_last = keys(visit(qi))
    out = jnp.zeros((tq, A_W), F32)
    for h in range(H_C):
        rs = slice(h * tq, (h + 1) * tq)
        acc = acc_ref[rs] + jnp.dot(p_ref[rs], k_last, preferred_element_type=F32)
        o_lat = (acc[:, 0:KV_LORA] / acc[:, DEN_LANE:DEN_LANE + 1]).astype(BF16)
        out = out + jnp.dot(o_lat, wuv_ref[h], preferred_element_type=F32)
    o_ref[0] = out


def _mla_attn(qcat, kcat, wuv_exp):
    b, s, _ = qcat.shape
    tq = min(s, 256)
    sub = min(tq, 256)
    rows = H_C * tq
    return pl.pallas_call(
        functools.partial(_mla_attn_kernel, tq=tq, sub=sub),
        grid=(b, s // tq),
        in_specs=[pl.BlockSpec((1, tq, H_C * QCAT_W), lambda i, j: (i, j, 0)),
                  pl.BlockSpec((1, s, QCAT_W), lambda i, j: (i, 0, 0)),
                  pl.BlockSpec(wuv_exp.shape, lambda i, j: (0, 0, 0))],
        out_specs=pl.BlockSpec((1, tq, A_W), lambda i, j: (i, j, 0)),
        out_shape=jax.ShapeDtypeStruct((b, s, A_W), F32),
        scratch_shapes=[pltpu.VMEM((rows, LANES), F32), pltpu.VMEM((rows, QCAT_W), F32),
                        pltpu.VMEM((rows, tq), BF16), pltpu.VMEM((rows, tq), F32)],
        compiler_params=_cparams(("parallel", "arbitrary")),
        name="mla_attn",
    )(qcat, kcat, wuv_exp)


def _moba_attn_kernel(q_ref, k_ref, v_ref, bt_ref, o_ref, km_ref, kb_ref, vb_ref, qa_ref,
                      m_ref, acc_ref, p_ref, s_ref, *, nb, sub):
    blk = MOBA_BLOCK
    qi = pl.program_id(2)
    scale = HEAD_DIM ** -0.5 * LOG2E
    lane = lax.broadcasted_iota(jnp.int32, (1, LANES), 1)
    lane_f = lane.astype(F32)

    @pl.when(qi == 0)
    def _():
        k = k_ref[0]
        v = v_ref[0]
        km_ref[...] = jnp.zeros_like(km_ref)
        for n in range(nb):
            km_ref[n:n + 1, :] = jnp.mean(k[n * blk:(n + 1) * blk, :], axis=0, keepdims=True)
        row_blk = lax.broadcasted_iota(jnp.int32, k.shape, 0) // blk
        for hh in range(2):
            hmask = (lane // HEAD_DIM) == hh
            other = lane - (1 - hh) * HEAD_DIM
            kb_ref[hh] = jnp.where(hmask, k, jnp.where(other == row_blk, 1.0, 0.0)).astype(BF16)
            vb_ref[hh] = jnp.where(hmask, v, 1.0).astype(BF16)

    q = q_ref[0]
    km = km_ref[...]
    for hh in range(2):
        hmask = (lane // HEAD_DIM) == hh
        qh = jnp.where(hmask, q, 0.0)
        gate = lax.dot_general(qh, km, _NT, precision=lax.Precision.HIGHEST, preferred_element_type=F32)
        gate = jnp.where(lane < qi, gate, NEG_INF)
        picked = jnp.zeros_like(gate)
        for _ in range(MOBA_TOPK):
            mx = jnp.max(gate, axis=-1, keepdims=True)
            idx = jnp.min(jnp.where(gate == mx, lane_f, 1e9), axis=-1, keepdims=True)
            hit = lane_f == idx
            picked = jnp.where(hit, 1.0, picked)
            gate = jnp.where(hit, BELOW_NEG_INF, gate)
        attend = ((picked > 0.0) & (lane < qi)) | (lane == qi)
        pen = jnp.where(attend, 0.0, NEG_INF)
        if hh == 0:
            pen = pltpu.roll(pen, HEAD_DIM, axis=1)
        qa_ref[hh] = jnp.where(hmask, qh * scale, pen).astype(BF16)

    tiles = [(hh, r) for hh in range(2) for r in range(blk // sub)]

    def rows(hh, r):
        return slice(hh * blk + r * sub, hh * blk + (r + 1) * sub)

    def scores(hh, r, j):
        tile = jnp.minimum(qi - j, 2)
        kb = kb_ref[hh, pl.ds(pl.multiple_of(j * blk, blk), blk), :]
        s = lax.dot_general(qa_ref[hh, r * sub:(r + 1) * sub, :], kb, _NT, preferred_element_type=F32)
        return s + bt_ref[hh, tile, r * sub:(r + 1) * sub, :]

    def weighted(hh, p, j):
        return jnp.dot(p, vb_ref[hh, pl.ds(pl.multiple_of(j * blk, blk), blk), :], preferred_element_type=F32)

    def visit(t):
        return jnp.where(t == 0, qi, t - 1)

    j_first = visit(jnp.minimum(1, qi))
    for hh, r in tiles:
        rs = rows(hh, r)
        s_own = scores(hh, r, qi)
        m_own = jnp.broadcast_to(jnp.max(s_own, axis=-1, keepdims=True), (sub, LANES))
        p_ref[rs] = jnp.exp2(s_own - jnp.concatenate([m_own] * (blk // LANES), axis=1)).astype(BF16)
        m_ref[rs] = m_own
        acc_ref[rs] = jnp.zeros((sub, LANES), F32)
        s_ref[rs] = scores(hh, r, j_first)

    def step(t, carry):
        j_next = visit(jnp.minimum(t + 1, qi))
        j_prev = visit(jnp.maximum(t - 1, 0))
        for hh, r in tiles:
            rs = rows(hh, r)
            s_next = scores(hh, r, j_next)
            pv = weighted(hh, p_ref[rs], j_prev)
            s_cur = s_ref[rs]
            m_i = m_ref[rs]
            m_new = jnp.maximum(m_i, jnp.max(s_cur, axis=-1, keepdims=True))
            p_ref[rs] = jnp.exp2(s_cur - jnp.concatenate([m_new] * (blk // LANES), axis=1)).astype(BF16)
            acc_ref[rs] = jnp.exp2(m_i - m_new) * (acc_ref[rs] + pv)
            s_ref[rs] = s_next
            m_ref[rs] = m_new
        return carry

    lax.fori_loop(1, qi + 1, step, 0)
    j_last = visit(qi)
    for r in range(blk // sub):
        o = []
        for hh in range(2):
            rs = rows(hh, r)
            acc = acc_ref[rs] + weighted(hh, p_ref[rs], j_last)
            o.append(acc / pltpu.roll(acc, HEAD_DIM, axis=1))
        o_ref[0, r * sub:(r + 1) * sub, :] = jnp.where(lane < HEAD_DIM, o[0], o[1])


def _moba_attn(q, k, v, btab):
    b, s, _ = q.shape
    blk = MOBA_BLOCK
    nb = s // blk
    hp = A_W // LANES
    assert nb <= HEAD_DIM
    sub = 256
    return pl.pallas_call(
        functools.partial(_moba_attn_kernel, nb=nb, sub=sub),
        grid=(b, hp, nb),
        in_specs=[pl.BlockSpec((1, blk, LANES), lambda i, h, j: (i, j, h)),
                  pl.BlockSpec((1, s, LANES), lambda i, h, j: (i, 0, h)),
                  pl.BlockSpec((1, s, LANES), lambda i, h, j: (i, 0, h)),
                  pl.BlockSpec((2, 3, blk, blk), lambda i, h, j: (h, 0, 0, 0))],
        out_specs=pl.BlockSpec((1, blk, LANES), lambda i, h, j: (i, j, h)),
        out_shape=jax.ShapeDtypeStruct((b, s, A_W), F32),
        scratch_shapes=[pltpu.VMEM((LANES, LANES), F32),
                        pltpu.VMEM((2, s, LANES), BF16),
                        pltpu.VMEM((2, s, LANES), BF16),
                        pltpu.VMEM((2, blk, LANES), BF16),
                        pltpu.VMEM((2 * blk, LANES), F32), pltpu.VMEM((2 * blk, LANES), F32),
                        pltpu.VMEM((2 * blk, blk), BF16), pltpu.VMEM((2 * blk, blk), F32)],
        compiler_params=_cparams(("parallel", "parallel", "arbitrary")),
        name="moba_attn",
    )(q, k, v, btab)


def _ret_kernel(r_ref, cos_ref, sin_ref, intra_ref, into_ref, outof_ref, carry_ref, bd_ref, gm_ref,
                gn_ref, o_ref, st_ref, s_scr):
    ci = pl.program_id(1)

    @pl.when(ci == 0)
    def _():
        s_scr[...] = jnp.zeros_like(s_scr)

    r = r_ref[0]
    cos = cos_ref[...]
    sin = sin_ref[...]

    def rope(x):
        return jnp.concatenate(
            [_rope_slab(x[:, s * LANES:(s + 1) * LANES], cos[:, s * LANES:(s + 1) * LANES],
                        sin[:, s * LANES:(s + 1) * LANES], HEAD_DIM) for s in range(R_W // LANES)], axis=1)

    q = rope(r[:, 0:R_W])
    k = rope(r[:, R_W:2 * R_W]) * (HEAD_DIM ** -0.5)
    v = r[:, 2 * R_W:3 * R_W]
    gate = r[:, 3 * R_W:]
    lane_head = lax.broadcasted_iota(jnp.int32, (1, R_W), 1) // HEAD_DIM
    kb = k.astype(BF16)
    state = s_scr[...]
    o = jnp.dot((q * into_ref[...]).astype(BF16), state.astype(BF16), preferred_element_type=F32)
    for h in range(H_B):
        qh = jnp.where(lane_head == h, q, 0.0).astype(BF16)
        a = lax.dot_general(qh, kb, _NT, preferred_element_type=F32) * intra_ref[h]
        vh = jnp.where(lane_head == h, v, 0.0).astype(BF16)
        o = o + jnp.dot(a.astype(BF16), vh, preferred_element_type=F32)
    kv = lax.dot_general((k * outof_ref[...]).astype(BF16), v.astype(BF16), _TN, preferred_element_type=F32)
    new_state = state * carry_ref[...] + kv * bd_ref[...]
    s_scr[...] = new_state
    st_ref[0] = new_state
    sq = o * o
    sq_hi = sq.astype(BF16)
    sq_lo = (sq - sq_hi.astype(F32)).astype(BF16)
    ms = (jnp.dot(sq_hi, gm_ref[...], preferred_element_type=F32)
          + jnp.dot(sq_lo, gm_ref[...], preferred_element_type=F32))
    ob = o * lax.rsqrt(ms + NORM_EPS) * gn_ref[...]
    o_ref[0] = ob * (gate * jax.nn.sigmoid(gate))


def _retention_tables(c):
    log_g = jnp.log(1.0 - 2.0 ** (-5.0 - jnp.arange(H_B, dtype=F32)))
    i = jnp.arange(c, dtype=F32)
    diff = i[:, None] - i[None, :]
    intra = jnp.where(diff >= 0, jnp.exp(log_g[:, None, None] * jnp.maximum(diff, 0.0)), 0.0)
    lg_lane = jnp.repeat(log_g, HEAD_DIM)
    into = jnp.exp(lg_lane[None, :] * (i[:, None] + 1.0))
    outof = jnp.exp(lg_lane[None, :] * (c - 1.0 - i[:, None]))
    carry = jnp.exp(lg_lane * c)[:, None]
    head = jnp.arange(R_W) // HEAD_DIM
    bd = (head[:, None] == head[None, :]).astype(F32)
    return intra, into, outof, carry, bd, (bd / HEAD_DIM).astype(BF16)


def _retention(r, cos, sin, gn):
    b, s, _ = r.shape
    c = min(s, 256)
    intra, into, outof, carry, bd, gm = _retention_tables(c)
    full = lambda a: pl.BlockSpec(a.shape, lambda i, j: (0,) * a.ndim)
    tab = pl.BlockSpec((c, R_W), lambda i, j: (j, 0))
    return pl.pallas_call(
        _ret_kernel,
        grid=(b, s // c),
        in_specs=[pl.BlockSpec((1, c, 4 * R_W), lambda i, j: (i, j, 0)), tab, tab,
                  full(intra), full(into), full(outof), full(carry), full(bd), full(gm), full(gn)],
        out_specs=[pl.BlockSpec((1, c, R_W), lambda i, j: (i, j, 0)),
                   pl.BlockSpec((1, R_W, R_W), lambda i, j: (i, 0, 0))],
        out_shape=[jax.ShapeDtypeStruct((b, s, R_W), F32), jax.ShapeDtypeStruct((b, R_W, R_W), F32)],
        scratch_shapes=[pltpu.VMEM((R_W, R_W), F32)],
        compiler_params=_cparams(("parallel", "arbitrary")),
        name="retention",
    )(r, cos, sin, intra, into, outof, carry, bd, gm, gn)


def _ret_step_kernel(q_ref, k_ref, v_ref, g_ref, st_ref, cos_ref, sin_ref, dec_ref, gn_ref, o_ref, ns_ref):
    half = HEAD_DIM // 2
    cos = cos_ref[...]
    sin = sin_ref[...]

    def rope_col(x):
        swapped = jnp.concatenate([x[:, half:, :], x[:, :half, :]], axis=1)
        return x * cos + swapped * sin

    q = rope_col(q_ref[0])
    k = rope_col(k_ref[0]) * (HEAD_DIM ** -0.5)
    v = v_ref[0]
    new_state = st_ref[0] * dec_ref[...] + k * v
    ns_ref[0] = new_state
    o = jnp.sum(q * new_state, axis=1, keepdims=True)
    ob = o * lax.rsqrt(jnp.mean(o * o, axis=-1, keepdims=True) + NORM_EPS) * gn_ref[...]
    gate = g_ref[0]
    o_ref[0] = ob * (gate * jax.nn.sigmoid(gate))


def _retention_step(r, state, pos, gn):
    b = r.shape[0]
    col = lambda x: x.reshape(b, H_B, HEAD_DIM, 1)
    rowv = lambda x: x.reshape(b, H_B, 1, HEAD_DIM)
    half = HEAD_DIM // 2
    inv = RET_ROPE_BASE ** (-jnp.arange(half, dtype=F32) / half)
    ang = jnp.asarray(pos, F32) * inv
    cos = jnp.concatenate([jnp.cos(ang), jnp.cos(ang)]).reshape(1, HEAD_DIM, 1)
    sin = jnp.concatenate([-jnp.sin(ang), jnp.sin(ang)]).reshape(1, HEAD_DIM, 1)
    log_g = jnp.log(1.0 - 2.0 ** (-5.0 - jnp.arange(H_B, dtype=F32)))
    dec = jnp.exp(log_g * 1.0).reshape(H_B, 1, 1)
    gn4 = gn.reshape(H_B, 1, HEAD_DIM)
    cspec = pl.BlockSpec((1, H_B, HEAD_DIM, 1), lambda i: (i, 0, 0, 0))
    rspec = pl.BlockSpec((1, H_B, 1, HEAD_DIM), lambda i: (i, 0, 0, 0))
    sspec = pl.BlockSpec((1, H_B, HEAD_DIM, HEAD_DIM), lambda i: (i, 0, 0, 0))
    full = lambda a: pl.BlockSpec(a.shape, lambda i: (0,) * a.ndim)
    o, ns = pl.pallas_call(
        _ret_step_kernel,
        grid=(b,),
        in_specs=[cspec, cspec, rspec, rspec, sspec, full(cos), full(sin), full(dec), full(gn4)],
        out_specs=[rspec, sspec],
        out_shape=[jax.ShapeDtypeStruct((b, H_B, 1, HEAD_DIM), F32),
                   jax.ShapeDtypeStruct((b, H_B, HEAD_DIM, HEAD_DIM), F32)],
        compiler_params=_cparams(("parallel",)),
        name="retention_step",
    )(col(r[:, 0:R_W]), col(r[:, R_W:2 * R_W]), rowv(r[:, 2 * R_W:3 * R_W]), rowv(r[:, 3 * R_W:]),
      state, cos, sin, dec, gn4)
    return o.reshape(b, R_W), ns


def _out_proj_kernel(x_ref, a_ref, b_ref, c_ref, wa_ref, wb_ref, wc_ref, y_ref):
    y = x_ref[...]
    y = y + jnp.dot(a_ref[...].astype(BF16), wa_ref[...], preferred_element_type=F32)
    y = y + jnp.dot(b_ref[...].astype(BF16), wb_ref[...], preferred_element_type=F32)
    y = y + jnp.dot(c_ref[...].astype(BF16), wc_ref[...], preferred_element_type=F32)
    y_ref[...] = y


def _out_proj(x, oa, ob, oc, wa, wb, wc):
    m = x.shape[0]
    tm = min(m, 512)
    row = lambda w_: pl.BlockSpec((tm, w_), lambda i: (i, 0))
    full = lambda a: pl.BlockSpec(a.shape, lambda i: (0,) * a.ndim)
    return pl.pallas_call(
        _out_proj_kernel,
        grid=(m // tm,),
        in_specs=[row(D_MODEL), row(A_W), row(R_W), row(A_W), full(wa), full(wb), full(wc)],
        out_specs=row(D_MODEL),
        out_shape=jax.ShapeDtypeStruct((m, D_MODEL), F32),
        compiler_params=_cparams(("parallel",)),
        name="out_proj",
    )(x, oa, ob, oc, wa, wb, wc)


_HALO = 16


def _ffn_kernel(pt_ref, xm_ref, xh_ref, st_ref, g2_ref, wup_ref, cw_ref, cb_ref, wdn_ref, fg_ref, kc_ref,
                y_ref, cn_ref, km_ref, hs_ref, u_ref, act_ref, kbuf, ksem, *, tm, nch, final, layer, sps):
    i = pl.program_id(1)
    step = pl.program_id(0) * pl.num_programs(1) + i
    nsteps = pl.num_programs(0) * pl.num_programs(1)
    nring, pps, page = kbuf.shape[0], kbuf.shape[1], kbuf.shape[-1]
    ppb = MOBA_BLOCK // page
    bps = pps // ppb
    lane3 = lax.broadcasted_iota(jnp.int32, (1, 1, LANES), 2)

    def kcopies(seq, sl):
        ring = (seq * sps + sl) % nring
        return [pltpu.make_async_copy(kc_ref.at[layer, pt_ref[seq, sl * pps + p]], kbuf.at[ring, p], ksem.at[ring])
                for p in range(pps)]

    def kstart(seq, sl):
        for p, cp in enumerate(kcopies(seq, sl)):
            cp.start(priority=p % 2)

    @pl.when(step == 0)
    def _():
        for sl in range(nring - 1):
            kstart(0, sl)

    km_ref[0] = jnp.zeros(km_ref.shape[1:], F32)

    def stream(sl):
        ahead = sl + nring - 1
        if ahead < sps:
            kstart(step, ahead)
        else:
            @pl.when(step + 1 < nsteps)
            def _():
                kstart(step + 1, ahead - sps)
        for cp in kcopies(step, sl):
            cp.wait()
        ring = (step * sps + sl) % nring
        km = km_ref[0]
        for n in range(bps):
            tot = kbuf[ring, n * ppb]
            for p in range(1, ppb):
                tot = tot + kbuf[ring, n * ppb + p]
            mean = jnp.sum(tot, axis=-1, keepdims=True) * (1.0 / MOBA_BLOCK)
            km = jnp.where(lane3 == sl * bps + n, mean, km)
        km_ref[0] = km

    g2 = g2_ref[...]
    hs_ref[0:_HALO, :] = _rms(xh_ref[0], g2).astype(BF16)
    hs_ref[_HALO:, :] = _rms(xm_ref[0], g2).astype(BF16)
    lo = _HALO - (CONV_W - 1)
    rc = min(tm, 128)
    row8 = lax.broadcasted_iota(jnp.int32, (8, FFN_CH), 0)
    from_state = (row8 >= 8 - (CONV_W - 1)) & (i == 0)

    def up(c, slot):
        for part in range(2):
            u_ref[slot, part] = jnp.dot(hs_ref[...], wup_ref[part, c], preferred_element_type=F32)
            edge = u_ref[slot, part, _HALO - 8:_HALO, :]
            u_ref[slot, part, _HALO - 8:_HALO, :] = jnp.where(from_state, st_ref[0, part, c], edge)

    def gate(c, slot):
        for r0 in range(0, tm, rc):
            convs = []
            for part in range(2):
                w = cw_ref[part, c]
                conv = cb_ref[part, c]
                for t in range(CONV_W):
                    conv = conv + w[t:t + 1, :] * u_ref[slot, part, lo + t + r0:lo + t + r0 + rc, :]
                convs.append(conv)
            act_ref[r0:r0 + rc, c * FFN_CH:(c + 1) * FFN_CH] = (
                (convs[0] * jax.nn.sigmoid(convs[0])) * convs[1]).astype(BF16)
        for part in range(2):
            cn_ref[0, part, c] = u_ref[slot, part, _HALO + tm - (CONV_W - 1):_HALO + tm, :]

    up(0, 0)
    for c in range(nch):
        if c < sps:
            stream(c)
        if c + 1 < nch:
            up(c + 1, (c + 1) & 1)
        gate(c, c & 1)
    y = xm_ref[0] + jnp.dot(act_ref[...], wdn_ref[...], preferred_element_type=F32)
    y_ref[0] = _rms(y, fg_ref[...]) if final else y


def _ffn(x, state5, g2, wup, cw, cb, wdn, fg, final, page_table, kcache_t, layer):
    b, s, _ = x.shape
    nch = wdn.shape[0]
    tm = min(s, 512)
    nt = s // tm
    hb = tm // _HALO
    nseq, npages = page_table.shape
    page = kcache_t.shape[-1]
    nbp = npages * page // MOBA_BLOCK
    sps = max(d for d in range(1, nch + 1) if nbp % d == 0)
    pps = npages // sps
    assert nseq == b * nt and nbp <= LANES and sps >= DMA_SLOTS - 1 and pps % (MOBA_BLOCK // page) == 0
    full = lambda a: pl.BlockSpec(a.shape, lambda i, j, pt: (0,) * a.ndim)
    state8 = jnp.pad(state5, ((0, 0), (0, 0), (0, 0), (8 - (CONV_W - 1), 0), (0, 0)))
    st_spec = lambda a: pl.BlockSpec((1,) + a.shape[1:], lambda i, j, pt: (i, 0, 0, 0, 0))
    wdn = wdn.reshape(nch * FFN_CH, D_MODEL)
    km_shape = (nseq, H_A, HEAD_DIM, LANES)
    grid_spec = pltpu.PrefetchScalarGridSpec(
        num_scalar_prefetch=1,
        grid=(b, nt),
        in_specs=[pl.BlockSpec((1, tm, D_MODEL), lambda i, j, pt: (i, j, 0)),
                  pl.BlockSpec((1, _HALO, D_MODEL), lambda i, j, pt: (i, jnp.maximum(j * hb - 1, 0), 0)),
                  st_spec(state8), full(g2), full(wup), full(cw), full(cb), full(wdn), full(fg),
                  pl.BlockSpec(memory_space=pl.ANY)],
        out_specs=[pl.BlockSpec((1, tm, D_MODEL), lambda i, j, pt: (i, j, 0)), st_spec(state5),
                   pl.BlockSpec((1,) + km_shape[1:], lambda i, j, pt: (i * nt + j, 0, 0, 0))],
        scratch_shapes=[pltpu.VMEM((tm + _HALO, D_MODEL), BF16),
                        pltpu.VMEM((2, 2, tm + _HALO, FFN_CH), F32),
                        pltpu.VMEM((tm, nch * FFN_CH), BF16),
                        pltpu.VMEM((DMA_SLOTS, pps, H_A, HEAD_DIM, page), F32),
                        pltpu.SemaphoreType.DMA((DMA_SLOTS,))],
    )
    return pl.pallas_call(
        functools.partial(_ffn_kernel, tm=tm, nch=nch, final=final, layer=layer, sps=sps),
        grid_spec=grid_spec,
        out_shape=[jax.ShapeDtypeStruct((b, s, D_MODEL), F32), jax.ShapeDtypeStruct(state5.shape, F32),
                   jax.ShapeDtypeStruct(km_shape, F32)],
        compiler_params=_cparams(("arbitrary", "arbitrary")),
        name="conv_ffn",
    )(page_table, x, x, state8, g2, wup, cw, cb, wdn, fg, kcache_t)


def _ffn_step_kernel(x_ref, st_ref, g2_ref, wup_ref, cw_ref, cb_ref, wdn_ref, fg_ref,
                     y_ref, cn_ref, hs_ref, *, final):
    c = pl.program_id(0)

    @pl.when(c == 0)
    def _():
        hs_ref[...] = _rms(x_ref[...], g2_ref[...]).astype(BF16)
        y_ref[...] = x_ref[...]

    convs = []
    for part in range(2):
        u = jnp.dot(hs_ref[...], wup_ref[part, 0], preferred_element_type=F32)
        w = cw_ref[part, 0]
        s0 = st_ref[0, part, 0]
        s1 = st_ref[0, part, 1]
        convs.append(cb_ref[part, 0] + w[0:1, :] * s0 + w[1:2, :] * s1 + w[2:3, :] * u)
        cn_ref[0, part, 0] = s1
        cn_ref[0, part, 1] = u
    act = (convs[0] * jax.nn.sigmoid(convs[0])) * convs[1]
    y_ref[...] += jnp.dot(act.astype(BF16), wdn_ref[0], preferred_element_type=F32)
    if final:
        @pl.when(c == pl.num_programs(0) - 1)
        def _():
            y_ref[...] = _rms(y_ref[...], fg_ref[...])


def _ffn_step(x, state5, g2, wup, cw, cb, wdn, fg, final):
    b = x.shape[0]
    nch = wdn.shape[0]
    full = lambda a: pl.BlockSpec(a.shape, lambda c: (0,) * a.ndim)
    st_spec = pl.BlockSpec((1,) + state5.shape[1:], lambda c: (c, 0, 0, 0, 0))
    return pl.pallas_call(
        functools.partial(_ffn_step_kernel, final=final),
        grid=(nch,),
        in_specs=[full(x), st_spec, full(g2),
                  pl.BlockSpec((2, 1, D_MODEL, FFN_CH), lambda c: (0, c, 0, 0)),
                  pl.BlockSpec((2, 1, CONV_W, FFN_CH), lambda c: (0, c, 0, 0)),
                  pl.BlockSpec((2, 1, 1, FFN_CH), lambda c: (0, c, 0, 0)),
                  pl.BlockSpec((1, FFN_CH, D_MODEL), lambda c: (c, 0, 0)),
                  full(fg)],
        out_specs=[full(x), st_spec],
        out_shape=[jax.ShapeDtypeStruct((b, D_MODEL), F32), jax.ShapeDtypeStruct(state5.shape, F32)],
        scratch_shapes=[pltpu.VMEM((b, D_MODEL), BF16)],
        compiler_params=_cparams(("arbitrary",)),
        name="conv_ffn_step",
    )(x, state5, g2, wup, cw, cb, wdn, fg)


def _moba_sel_kernel(q_ref, km_ref, sel_ref, *, nbp):
    lane = lax.broadcasted_iota(jnp.int32, (1, 1, LANES), 2)
    gate = jnp.sum(km_ref[0] * q_ref[0], axis=1, keepdims=True)
    lane_f = lane.astype(F32)
    gate = jnp.where(lane < nbp, gate, BELOW_NEG_INF)
    out = jnp.zeros_like(gate)
    for t in range(MOBA_TOPK):
        mx = jnp.max(gate, axis=-1, keepdims=True)
        idx = jnp.min(jnp.where(gate == mx, lane_f, 1e9), axis=-1, keepdims=True)
        out = jnp.where(lane == t, idx, out)
        gate = jnp.where(lane_f == idx, BELOW_NEG_INF, gate)
    sel_ref[0] = out.astype(jnp.int32)


def _moba_select(q, kmean, nbp):
    b = q.shape[0]
    sel = pl.pallas_call(
        functools.partial(_moba_sel_kernel, nbp=nbp),
        grid=(b,),
        in_specs=[pl.BlockSpec((1, H_A, HEAD_DIM, 1), lambda i: (i, 0, 0, 0)),
                  pl.BlockSpec((1, H_A, HEAD_DIM, LANES), lambda i: (i, 0, 0, 0))],
        out_specs=pl.BlockSpec((1, H_A, 1, LANES), lambda i: (i, 0, 0, 0)),
        out_shape=jax.ShapeDtypeStruct((b, H_A, 1, LANES), jnp.int32),
        compiler_params=_cparams(("parallel",)),
        name="moba_select",
    )(q.reshape(b, H_A, HEAD_DIM, 1), kmean)
    return sel[:, :, 0, :MOBA_TOPK]


def _moba_dec_kernel(pt_ref, sel_ref, q_ref, kn_ref, vn_ref, bl_ref, bs_ref, kc_ref, vc_ref, o_ref,
                     kbuf, vbuf, sem, *, layer, nbp, ppb):
    b = pl.program_id(0)
    nseq = pl.num_programs(0)
    scale = HEAD_DIM ** -0.5
    npg = MOBA_TOPK * ppb

    def copies(bb, slot):
        out = []
        for h in range(H_A):
            for t in range(MOBA_TOPK):
                n = sel_ref[bb, h * MOBA_TOPK + t]
                for p in range(ppb):
                    pg = pt_ref[bb, n * ppb + p]
                    out.append(pltpu.make_async_copy(kc_ref.at[layer, pg, h], kbuf.at[slot, h, t * ppb + p],
                                                     sem.at[0, slot]))
                    out.append(pltpu.make_async_copy(vc_ref.at[layer, pg, h], vbuf.at[slot, h, t * ppb + p],
                                                     sem.at[1, slot]))
        return out

    @pl.when(b == 0)
    def _():
        for cp in copies(0, 0):
            cp.start()

    slot = b & 1

    @pl.when(b + 1 < nseq)
    def _():
        for cp in copies(b + 1, 1 - slot):
            cp.start()

    for cp in copies(b, slot):
        cp.wait()

    for h in range(H_A):
        qh = q_ref[0, h] * scale
        far = bs_ref[h, 0]
        rows = []
        for t in range(MOBA_TOPK):
            is_last = sel_ref[b, h * MOBA_TOPK + t] == nbp - 1
            for p in range(ppb):
                s = jnp.sum(kbuf[slot, h, t * ppb + p] * qh, axis=0, keepdims=True)
                rows.append(s + jnp.where(is_last, bl_ref[h, p:p + 1, :], far))
        s_sel = jnp.concatenate(rows, axis=0)
        s_new = jnp.sum(kn_ref[0, h] * qh, axis=0, keepdims=True) + bs_ref[h, 1]
        mx = jnp.maximum(jnp.max(jnp.max(s_sel, axis=1, keepdims=True), axis=0, keepdims=True), s_new)
        p_sel = jnp.exp(s_sel - mx)
        p_new = jnp.exp(s_new - mx)
        denom = jnp.sum(jnp.sum(p_sel, axis=1, keepdims=True), axis=0, keepdims=True) + p_new
        pv = vbuf[slot, h, 0] * p_sel[0:1, :]
        for g in range(1, npg):
            pv = pv + vbuf[slot, h, g] * p_sel[g:g + 1, :]
        o = jnp.sum(pv, axis=1, keepdims=True) + p_new * vn_ref[0, h]
        o_ref[0, h] = o / denom


def _t5_bucket(dist):
    n = jnp.maximum(dist, 0)
    max_exact = T5_BUCKETS // 2
    nf = jnp.maximum(n, 1).astype(F32)
    large = max_exact + jnp.floor(jnp.log(nf / max_exact) / math.log(T5_MAX_DIST / max_exact)
                                  * (T5_BUCKETS - max_exact))
    large = jnp.minimum(large, float(T5_BUCKETS - 1))
    return jnp.where(n < max_exact, n.astype(F32), large)


def _moba_decode(page_table, sel, q, k_new, v_new, kcache_t, vcache_t, layer, rel_bias, past_len):
    b, npages = page_table.shape
    page = kcache_t.shape[-1]
    ppb = MOBA_BLOCK // page
    nbp = past_len // MOBA_BLOCK
    off = jnp.arange(MOBA_BLOCK)
    dist_last = past_len - ((nbp - 1) * MOBA_BLOCK + off)
    b_last = _bias_lookup(rel_bias, _t5_bucket(dist_last)).reshape(H_A, ppb, page)
    dist_far = jnp.asarray([past_len - (nbp - 1) * MOBA_BLOCK + 1, 0])
    b_sc = _bias_lookup(rel_bias, _t5_bucket(dist_far))
    sel_flat = sel.reshape(b, H_A * MOBA_TOPK)
    col = lambda x: x.reshape(b, H_A, HEAD_DIM, 1)
    cspec = pl.BlockSpec((1, H_A, HEAD_DIM, 1), lambda i, pt, sl: (i, 0, 0, 0))
    grid_spec = pltpu.PrefetchScalarGridSpec(
        num_scalar_prefetch=2,
        grid=(b,),
        in_specs=[cspec, cspec, cspec,
                  pl.BlockSpec(b_last.shape, lambda i, pt, sl: (0, 0, 0)),
                  pl.BlockSpec(memory_space=pltpu.SMEM),
                  pl.BlockSpec(memory_space=pl.ANY),
                  pl.BlockSpec(memory_space=pl.ANY)],
        out_specs=cspec,
        scratch_shapes=[pltpu.VMEM((2, H_A, MOBA_TOPK * ppb, HEAD_DIM, page), F32),
                        pltpu.VMEM((2, H_A, MOBA_TOPK * ppb, HEAD_DIM, page), F32),
                        pltpu.SemaphoreType.DMA((2, 2))],
    )
    o = pl.pallas_call(
        functools.partial(_moba_dec_kernel, layer=layer, nbp=nbp, ppb=ppb),
        grid_spec=grid_spec,
        out_shape=jax.ShapeDtypeStruct((b, H_A, HEAD_DIM, 1), F32),
        compiler_params=_cparams(("arbitrary",)),
        name="moba_decode",
    )(page_table, sel_flat, col(q), col(k_new), col(v_new), b_last, b_sc, kcache_t, vcache_t)
    return o.reshape(b, A_W)


def _mla_dec_kernel(pt_ref, q_ref, kn_ref, mc_ref, o_ref, buf, sem, *, layer, pc):
    b = pl.program_id(0)
    nseq = pl.num_programs(0)
    npages = pt_ref.shape[1]
    nchunks = npages // pc
    q = q_ref[0]
    q_lat = q[:, 0:KV_LORA]
    q_rope = q[:, KV_LORA:KV_LORA + D_ROPE]

    nslots = buf.shape[0]
    total = nseq * nchunks

    def copies(g):
        bb = g // nchunks
        c = g % nchunks
        slot = g % nslots
        return [pltpu.make_async_copy(mc_ref.at[layer, pt_ref[bb, c * pc + p]], buf.at[slot, p], sem.at[slot])
                for p in range(pc)]

    def start(g):
        for p, cp in enumerate(copies(g)):
            cp.start(priority=p % 2)

    @pl.when(b == 0)
    def _():
        for g in range(nslots - 1):
            start(g)

    def chunk(c, carry):
        m_i, l_i, acc = carry
        g = b * nchunks + c
        slot = g % nslots

        @pl.when(g + nslots - 1 < total)
        def _():
            start(g + nslots - 1)

        for cp in copies(g):
            cp.wait()
        kv = jnp.concatenate([buf[slot, p] for p in range(pc)], axis=1).astype(BF16)
        lat = kv[0:KV_LORA]
        s = (jnp.dot(q_lat, lat, preferred_element_type=F32)
             + jnp.dot(q_rope, kv[KV_LORA:], preferred_element_type=F32))
        m_new = jnp.maximum(m_i, jnp.max(s, axis=-1, keepdims=True))
        alpha = jnp.exp2(m_i - m_new)
        p = jnp.exp2(s - m_new)
        l_new = alpha * l_i + jnp.sum(p, axis=-1, keepdims=True)
        acc_new = alpha * acc + lax.dot_general(p.astype(BF16), lat, _NT, preferred_element_type=F32)
        return m_new, l_new, acc_new

    init = (jnp.full((8, 1), NEG_INF, F32), jnp.zeros((8, 1), F32), jnp.zeros((8, KV_LORA), F32))
    m_i, l_i, acc = lax.fori_loop(0, nchunks, chunk, init)
    kn = kn_ref[0].astype(F32)
    s_new = jnp.sum(q.astype(F32) * kn, axis=-1, keepdims=True)
    m_new = jnp.maximum(m_i, s_new)
    alpha = jnp.exp2(m_i - m_new)
    p_new = jnp.exp2(s_new - m_new)
    l_fin = alpha * l_i + p_new
    acc_fin = alpha * acc + p_new * kn[:, 0:KV_LORA]
    o_ref[0] = acc_fin / l_fin


def _mla_decode(page_table, qcat, kcat_new, mcache_t, layer):
    b, npages = page_table.shape
    page = mcache_t.shape[-1]
    pc = min(npages, 16)
    q8 = jnp.pad(qcat.reshape(b, H_C, QCAT_W), ((0, 0), (0, 8 - H_C), (0, 0)))
    grid_spec = pltpu.PrefetchScalarGridSpec(
        num_scalar_prefetch=1,
        grid=(b,),
        in_specs=[pl.BlockSpec((1, 8, QCAT_W), lambda i, pt: (i, 0, 0)),
                  pl.BlockSpec((1, 1, QCAT_W), lambda i, pt: (i, 0, 0)),
                  pl.BlockSpec(memory_space=pl.ANY)],
        out_specs=pl.BlockSpec((1, 8, KV_LORA), lambda i, pt: (i, 0, 0)),
        scratch_shapes=[pltpu.VMEM((DMA_SLOTS, pc, KV_LORA + D_ROPE, page), F32),
                        pltpu.SemaphoreType.DMA((DMA_SLOTS,))],
    )
    o_lat = pl.pallas_call(
        functools.partial(_mla_dec_kernel, layer=layer, pc=pc),
        grid_spec=grid_spec,
        out_shape=jax.ShapeDtypeStruct((b, 8, KV_LORA), F32),
        compiler_params=_cparams(("arbitrary",)),
        name="mla_decode",
    )(page_table, q8, kcat_new.reshape(b, 1, QCAT_W), mcache_t)
    return o_lat[:, :H_C, :].reshape(b, H_C * KV_LORA)


def _matmul_kernel(a_ref, w_ref, o_ref):
    o_ref[...] = jnp.dot(a_ref[...].astype(BF16), w_ref[...], preferred_element_type=F32)


def _matmul(a, w):
    full = lambda x: pl.BlockSpec(x.shape, lambda: (0,) * x.ndim)
    return pl.pallas_call(
        _matmul_kernel,
        in_specs=[full(a), full(w)],
        out_specs=pl.BlockSpec((a.shape[0], w.shape[1]), lambda: (0, 0)),
        out_shape=jax.ShapeDtypeStruct((a.shape[0], w.shape[1]), F32),
        name="latent_out",
    )(a, w)


def _rope_tables(pos, base, group, width):
    half = group // 2
    inv = base ** (-jnp.arange(half, dtype=F32) / half)
    ang = pos.astype(F32)[:, None] * inv[None, :]
    cos = jnp.concatenate([jnp.cos(ang), jnp.cos(ang)], axis=1)
    sin = jnp.concatenate([-jnp.sin(ang), jnp.sin(ang)], axis=1)
    reps = width // group
    return jnp.tile(cos, (1, reps)), jnp.tile(sin, (1, reps))


def _bias_lookup(rel_bias, buckets):
    onehot = (buckets[..., None] == jnp.arange(T5_BUCKETS, dtype=F32)).astype(F32)
    out = jnp.einsum("...k,kh->...h", onehot, rel_bias.astype(F32), precision=lax.Precision.HIGHEST)
    return jnp.moveaxis(out, -1, 0)


def _moba_bias_tiles(rel_bias):
    r = jnp.arange(MOBA_BLOCK)
    d0 = r[:, None] - r[None, :]
    own = jnp.where((d0 >= 0)[None], _bias_lookup(rel_bias, _t5_bucket(d0)), NEG_INF)
    prev = _bias_lookup(rel_bias, _t5_bucket(d0 + MOBA_BLOCK))
    far = jnp.broadcast_to(_bias_lookup(rel_bias, _t5_bucket(jnp.asarray([MOBA_BLOCK + 1])))[:, :, None], own.shape)
    return jnp.stack([own, prev, far], axis=1) * LOG2E


def _layer_weights(w_in, w_uq, w_uk, w_uv, w_o, w_up, conv_w, conv_b, w_down):
    d_ff = w_down.shape[0]
    nch = d_ff // FFN_CH
    w = {}
    w["in"] = jnp.pad(w_in, ((0, 0), (0, N_IN_PAD - N_IN))).astype(BF16)
    uq = w_uq.reshape(Q_LORA, H_C, D_NOPE + D_ROPE)
    w["uq_nope"] = uq[:, :, :D_NOPE].reshape(Q_LORA, H_C * D_NOPE).astype(BF16)
    w["uq_rope"] = jnp.pad(uq[:, :, D_NOPE:].reshape(Q_LORA, H_C * D_ROPE),
                           ((0, 0), (0, 2 * LANES - H_C * D_ROPE))).astype(BF16)
    eye = jnp.eye(H_C, dtype=F32)
    w["uk_bd"] = jnp.einsum("rhd,hg->hdgr", w_uk, eye).reshape(H_C * D_NOPE, H_C * KV_LORA).astype(BF16)
    w["uv_exp"] = jnp.einsum("rhd,hg->hrgd", w_uv, eye).reshape(H_C, KV_LORA, H_C * HEAD_DIM).astype(BF16)
    w["o_a"] = w_o[0:A_W].astype(BF16)
    w["o_b"] = w_o[A_W:A_W + R_W].astype(BF16)
    w["o_c"] = w_o[A_W + R_W:].astype(BF16)
    w["up"] = w_up.reshape(D_MODEL, 2, nch, FFN_CH).transpose(1, 2, 0, 3).astype(BF16)
    w["conv_w"] = conv_w.reshape(CONV_W, 2, nch, FFN_CH).transpose(1, 2, 0, 3)
    w["conv_b"] = conv_b.reshape(2, nch, 1, FFN_CH)
    w["down"] = w_down.reshape(nch, FFN_CH, D_MODEL).astype(BF16)
    return w


def _state_to_chunks(state, nch):
    b = state.shape[0]
    return state.reshape(b, CONV_W - 1, 2, nch, FFN_CH).transpose(0, 2, 3, 1, 4)


def _chunks_to_state(c5):
    b, _, nch, rows, ch = c5.shape
    return c5.transpose(0, 3, 1, 2, 4).reshape(b, rows, 2 * nch * ch)


def _ret_state_blocks(sfull):
    b = sfull.shape[0]
    s6 = sfull.reshape(b, H_B, HEAD_DIM, H_B, HEAD_DIM)
    return jnp.stack([s6[:, h, :, h, :] for h in range(H_B)], axis=1)


def _prompt_layer(x, w, g1, gq, gkv, gret, g2, fg, btab, tabs, final, page_table, kcache_t, layer):
    b, s, _ = x.shape
    m = b * s
    x2 = x.reshape(m, D_MODEL)
    a_q, a_k, a_v, r, mla_new, kcat, qcat = _in_proj(x2, g1, w["in"], gq, gkv, w["uq_nope"], w["uq_rope"], w["uk_bd"],
                                                     tabs["mla_cos"], tabs["mla_sin"], s)
    o_a = _moba_attn(a_q.reshape(b, s, A_W), a_k.reshape(b, s, A_W), a_v.reshape(b, s, A_W), btab)
    o_b, sfull = _retention(r.reshape(b, s, 4 * R_W), tabs["ret_cos"], tabs["ret_sin"], gret)
    o_c = _mla_attn(qcat.reshape(b, s, H_C * QCAT_W), kcat.reshape(b, s, QCAT_W), w["uv_exp"])
    x1 = _out_proj(x2, o_a.reshape(m, A_W), o_b.reshape(m, R_W), o_c.reshape(m, A_W), w["o_a"], w["o_b"], w["o_c"])
    nch = w["down"].shape[0]
    st0 = jnp.zeros((b, 2, nch, CONV_W - 1, FFN_CH), F32)
    y, cn, kmean = _ffn(x1.reshape(b, s, D_MODEL), st0, g2, w["up"], w["conv_w"], w["conv_b"], w["down"], fg, final,
                        page_table, kcache_t, layer)
    return (y, kmean, a_k.reshape(b, s, H_A, HEAD_DIM), a_v.reshape(b, s, H_A, HEAD_DIM),
            mla_new.reshape(b, s, KV_LORA + D_ROPE), _ret_state_blocks(sfull), _chunks_to_state(cn))


def _sample_layer(x, w, g1, gq, gkv, gret, g2, fg, rel_bias, tabs, final,
                  kmean, kcache_t, vcache_t, mcache_t, layer, ret_state, conv_state, page_table, past_len):
    b = x.shape[0]
    a_q, a_k, a_v, r, mla_new, kcat, qcat = _in_proj(x, g1, w["in"], gq, gkv, w["uq_nope"], w["uq_rope"], w["uk_bd"],
                                                     tabs["mla_cos_s"], tabs["mla_sin_s"], 1)
    nbp = past_len // MOBA_BLOCK
    sel = _moba_select(a_q, kmean, nbp)
    o_a = _moba_decode(page_table, sel, a_q, a_k, a_v, kcache_t, vcache_t, layer, rel_bias, past_len)
    o_b, ret_new = _retention_step(r, ret_state, past_len, gret)
    o_lat = _mla_decode(page_table, qcat, kcat, mcache_t, layer)
    o_c = _matmul(o_lat, w["uv_exp"].reshape(H_C * KV_LORA, A_W))
    x1 = _out_proj(x, o_a, o_b, o_c, w["o_a"], w["o_b"], w["o_c"])
    nch = w["down"].shape[0]
    st = _state_to_chunks(conv_state, nch).transpose(2, 1, 3, 0, 4)
    y, cn = _ffn_step(x1, st, g2, w["up"], w["conv_w"], w["conv_b"], w["down"], fg, final)
    conv_new = _chunks_to_state(cn.transpose(3, 1, 0, 2, 4))
    return (y, a_k.reshape(b, 1, H_A, HEAD_DIM), a_v.reshape(b, 1, H_A, HEAD_DIM),
            mla_new.reshape(b, 1, KV_LORA + D_ROPE), ret_new, conv_new)


def kernel(x_prompt, x_sample, cache_moba_k, cache_moba_v, cache_mla, state_ret, state_conv, page_table, norm1_g, w_in, rel_bias, ret_norm_g, mla_q_norm_g, mla_kv_norm_g, w_uq, w_uk, w_uv, w_o, norm2_g, w_up, conv_w, conv_b, w_down, final_norm_g):
    depth = w_in.shape[0]
    bp, s, _ = x_prompt.shape
    bs, s_dec, _ = x_sample.shape
    n_pool, page = cache_moba_k.shape[1], cache_moba_k.shape[2]
    past_len = page_table.shape[1] * page
    assert s % MOBA_BLOCK == 0 and s_dec == 1 and MOBA_BLOCK % page == 0
    assert past_len % MOBA_BLOCK == 0 and past_len // MOBA_BLOCK >= MOBA_TOPK
    assert w_down.shape[1] % FFN_CH == 0

    pos_p = jnp.arange(s)
    pos_s = jnp.full((bs,), past_len)
    tabs = {}
    tabs["ret_cos"], tabs["ret_sin"] = _rope_tables(pos_p, RET_ROPE_BASE, HEAD_DIM, R_W)
    tabs["mla_cos"], tabs["mla_sin"] = _rope_tables(pos_p, MLA_ROPE_BASE, D_ROPE, LANES)
    tabs["mla_cos_s"], tabs["mla_sin_s"] = _rope_tables(pos_s, MLA_ROPE_BASE, D_ROPE, LANES)
    btab = _moba_bias_tiles(rel_bias)
    fg = final_norm_g.reshape(1, D_MODEL)

    kcache_t = jnp.transpose(cache_moba_k, (0, 1, 3, 4, 2))
    vcache_t = jnp.transpose(cache_moba_v, (0, 1, 3, 4, 2))
    mcache_t = jnp.transpose(cache_mla, (0, 1, 3, 2))

    xp = x_prompt
    xs = x_sample.reshape(bs, D_MODEL)
    outs_p, outs_s = [], []
    for l in range(depth):
        final = l == depth - 1
        w = _layer_weights(w_in[l], w_uq[l], w_uk[l], w_uv[l], w_o[l], w_up[l], conv_w[l], conv_b[l], w_down[l])
        norms = (norm1_g[l].reshape(1, -1), mla_q_norm_g[l].reshape(1, -1), mla_kv_norm_g[l].reshape(1, -1),
                 ret_norm_g[l].reshape(1, -1), norm2_g[l].reshape(1, -1), fg)
        xp, kmean, *rest_p = _prompt_layer(xp, w, *norms, btab, tabs, final, page_table, kcache_t, l)
        outs_p.append(rest_p)
        xs, *rest_s = _sample_layer(
            xs, w, *norms, rel_bias, tabs, final,
            kmean, kcache_t, vcache_t, mcache_t, l, state_ret[l], state_conv[l], page_table, past_len)
        outs_s.append(rest_s)
    stack = lambda outs, i: jnp.stack([o[i] for o in outs])
    return (xp, xs.reshape(bs, 1, D_MODEL),
            stack(outs_p, 0), stack(outs_p, 1), stack(outs_p, 2), stack(outs_p, 3), stack(outs_p, 4),
            stack(outs_s, 0), stack(outs_s, 1), stack(outs_s, 2), stack(outs_s, 3), stack(outs_s, 4))
```

```python
import functools
import math

import jax
import jax.numpy as jnp
from jax import lax
from jax.experimental import pallas as pl
from jax.experimental.pallas import tpu as pltpu

F32 = jnp.float32
BF16 = jnp.bfloat16

D_MODEL = 1024
HEAD_DIM = 64
H_A = 6
H_B = 4
H_C = 6
MOBA_BLOCK = 256
MOBA_TOPK = 3
T5_BUCKETS = 32
T5_MAX_DIST = 128
RET_ROPE_BASE = 10000.0
Q_LORA = 256
KV_LORA = 128
D_NOPE = 64
D_ROPE = 32
MLA_ROPE_BASE = 10000.0
CONV_W = 3
NORM_EPS = 1e-6
NEG_INF = -1e30
BELOW_NEG_INF = -3.0e38
LOG2E = 1.4426950408889634
DEN_LANE = KV_LORA + D_ROPE

LANES = 128
A_W = H_A * HEAD_DIM
R_W = H_B * HEAD_DIM
N_IN = 3 * A_W + 4 * R_W + Q_LORA + KV_LORA + D_ROPE
N_IN_PAD = -(-N_IN // LANES) * LANES
M_W = N_IN_PAD - 3 * A_W - 4 * R_W
QCAT_W = 2 * LANES
FFN_CH = 256
VMEM_LIMIT = 56 * 1024 * 1024
DMA_SLOTS = 4

_NT = (((1,), (1,)), ((), ()))
_TN = (((0,), (0,)), ((), ()))


def _cparams(sem):
    return pltpu.CompilerParams(dimension_semantics=sem, vmem_limit_bytes=VMEM_LIMIT)


def _rms(x, g):
    return x * lax.rsqrt(jnp.mean(x * x, axis=-1, keepdims=True) + NORM_EPS) * g


def _rope_slab(x, cos, sin_signed, group):
    half = group // 2
    lane = lax.broadcasted_iota(jnp.int32, (1, LANES), 1)
    fwd = pltpu.roll(x, LANES - half, axis=1)
    bwd = pltpu.roll(x, half, axis=1)
    swapped = jnp.where((lane & (group - 1)) < half, fwd, bwd)
    return x * cos + swapped * sin_signed


def _in_proj_kernel(x_ref, g_ref, w_ref, gq_ref, gkv_ref, wn_ref, wr_ref, wuk_ref, cos_ref, sin_ref,
                    q_ref, k_ref, v_ref, r_ref, mla_ref, kcat_ref, qcat_ref):
    h = _rms(x_ref[...], g_ref[...]).astype(BF16)
    y = jnp.dot(h, w_ref[...], preferred_element_type=F32)
    q_ref[...] = y[:, 0:A_W]
    k_ref[...] = y[:, A_W:2 * A_W]
    v_ref[...] = y[:, 2 * A_W:3 * A_W]
    r_ref[...] = y[:, 3 * A_W:3 * A_W + 4 * R_W]
    _mla_prep_rows(y[:, 3 * A_W + 4 * R_W:], gq_ref, gkv_ref, wn_ref, wr_ref, wuk_ref, cos_ref, sin_ref,
                   mla_ref, kcat_ref, qcat_ref)


def _in_proj(x, g, w, gq, gkv, wn, wr, wuk, cos, sin, rows_per_seq):
    m = x.shape[0]
    tm = min(m, 512)
    row = lambda w_: pl.BlockSpec((tm, w_), lambda i: (i, 0))
    full = lambda a: pl.BlockSpec(a.shape, lambda i: (0,) * a.ndim)
    if cos.shape[0] == rows_per_seq and rows_per_seq >= tm:
        nt = rows_per_seq // tm
        tab = pl.BlockSpec((tm, LANES), lambda i: (i % nt, 0))
    else:
        tab = pl.BlockSpec((tm, LANES), lambda i: (0, 0))
    widths = (A_W, A_W, A_W, 4 * R_W, KV_LORA + D_ROPE, QCAT_W, H_C * QCAT_W)
    dtypes = (F32, F32, F32, F32, F32, BF16, BF16)
    return pl.pallas_call(
        _in_proj_kernel,
        grid=(m // tm,),
        in_specs=[row(D_MODEL), full(g), full(w), full(gq), full(gkv), full(wn), full(wr), full(wuk), tab, tab],
        out_specs=[row(w_) for w_ in widths],
        out_shape=[jax.ShapeDtypeStruct((m, w_), dt) for w_, dt in zip(widths, dtypes)],
        compiler_params=_cparams(("parallel",)),
        name="in_proj",
    )(x, g, w, gq, gkv, wn, wr, wuk, cos, sin)


def _mla_prep_rows(m, gq_ref, gkv_ref, wn_ref, wr_ref, wuk_ref, cos_ref, sin_ref, mla_ref, kcat_ref, qcat_ref):
    scale = (D_NOPE + D_ROPE) ** -0.5 * LOG2E
    cos = cos_ref[...]
    sin = sin_ref[...]
    lane = lax.broadcasted_iota(jnp.int32, (1, LANES), 1)
    ckv = _rms(m[:, Q_LORA:Q_LORA + KV_LORA], gkv_ref[...])
    kr = _rope_slab(m[:, Q_LORA + KV_LORA:], cos, sin, D_ROPE)
    mla_ref[:, 0:KV_LORA] = ckv
    mla_ref[:, KV_LORA:] = kr[:, 0:D_ROPE]
    kcat_ref[:, 0:LANES] = ckv.astype(BF16)
    kcat_ref[:, LANES:] = jnp.where(lane < D_ROPE, kr, jnp.where(lane == D_ROPE, 1.0, 0.0)).astype(BF16)
    cq = _rms(m[:, 0:Q_LORA], gq_ref[...]).astype(BF16)
    q_nope = jnp.dot(cq, wn_ref[...], preferred_element_type=F32)
    q_rope = jnp.dot(cq, wr_ref[...], preferred_element_type=F32)
    q_lat = jnp.dot(q_nope.astype(BF16), wuk_ref[...], preferred_element_type=F32)
    per_slab = LANES // D_ROPE
    for h in range(H_C):
        s = h // per_slab
        rot = _rope_slab(q_rope[:, s * LANES:(s + 1) * LANES], cos, sin, D_ROPE)
        sh = (h % per_slab) * D_ROPE
        piece = rot if sh == 0 else pltpu.roll(rot, LANES - sh, axis=1)
        piece = jnp.where(lane < D_ROPE, piece, 0.0) * scale
        qcat_ref[:, h * QCAT_W:h * QCAT_W + LANES] = (q_lat[:, h * LANES:(h + 1) * LANES] * scale).astype(BF16)
        qcat_ref[:, h * QCAT_W + LANES:(h + 1) * QCAT_W] = piece.astype(BF16)


def _mla_attn_kernel(q_ref, k_ref, wuv_ref, o_ref, m_ref, acc_ref, p_ref, s_ref, *, tq, sub):
    qi = pl.program_id(1)
    tiles = [(h, r) for h in range(H_C) for r in range(tq // sub)]

    def keys(j):
        return k_ref[0, pl.ds(pl.multiple_of(j * tq, tq), tq), :]

    def rows(h, r):
        return slice(h * tq + r * sub, h * tq + (r + 1) * sub)

    def scores(h, r, kb):
        return lax.dot_general(q_ref[0, r * sub:(r + 1) * sub, h * QCAT_W:(h + 1) * QCAT_W], kb, _NT,
                               preferred_element_type=F32)

    def visit(t):
        return jnp.where(t == 0, qi, t - 1)

    k_diag = keys(qi)
    k_first = keys(visit(jnp.minimum(1, qi)))
    col = lax.broadcasted_iota(jnp.int32, (sub, tq), 1)
    row = lax.broadcasted_iota(jnp.int32, (sub, tq), 0)
    for h, r in tiles:
        rs = rows(h, r)
        s_diag = jnp.where(col <= row + r * sub, scores(h, r, k_diag), NEG_INF)
        m_diag = jnp.broadcast_to(jnp.max(s_diag, axis=-1, keepdims=True), (sub, LANES))
        p_ref[rs] = jnp.exp2(s_diag - jnp.concatenate([m_diag] * (tq // LANES), axis=1)).astype(BF16)
        m_ref[rs] = m_diag
        acc_ref[rs] = jnp.zeros((sub, QCAT_W), F32)
        s_ref[rs] = scores(h, r, k_first)

    def step(t, carry):
        k_next = keys(visit(jnp.minimum(t + 1, qi)))
        k_prev = keys(visit(jnp.maximum(t - 1, 0)))
        for h, r in tiles:
            rs = rows(h, r)
            s_next = scores(h, r, k_next)
            pv = jnp.dot(p_ref[rs], k_prev, preferred_element_type=F32)
            s_cur = s_ref[rs]
            m_i = m_ref[rs]
            m_new = jnp.maximum(m_i, jnp.max(s_cur, axis=-1, keepdims=True))
            alpha = jnp.exp2(m_i - m_new)
            p_ref[rs] = jnp.exp2(s_cur - jnp.concatenate([m_new] * (tq // LANES), axis=1)).astype(BF16)
            acc_ref[rs] = jnp.concatenate([alpha] * (QCAT_W // LANES), axis=1) * (acc_ref[rs] + pv)
            s_ref[rs] = s_next
            m_ref[rs] = m_new
        return carry

    lax.fori_loop(1, qi + 1, step, 0)
    k_last = keys(visit(qi))
    out = jnp.zeros((tq, A_W), F32)
    for h in range(H_C):
        rs = slice(h * tq, (h + 1) * tq)
        acc = acc_ref[rs] + jnp.dot(p_ref[rs], k_last, preferred_element_type=F32)
        o_lat = (acc[:, 0:KV_LORA] / acc[:, DEN_LANE:DEN_LANE + 1]).astype(BF16)
        out = out + jnp.dot(o_lat, wuv_ref[h], preferred_element_type=F32)
    o_ref[0] = out


def _mla_attn(qcat, kcat, wuv_exp):
    b, s, _ = qcat.shape
    tq = min(s, 256)
    sub = min(tq, 256)
    rows = H_C * tq
    return pl.pallas_call(
        functools.partial(_mla_attn_kernel, tq=tq, sub=sub),
        grid=(b, s // tq),
        in_specs=[pl.BlockSpec((1, tq, H_C * QCAT_W), lambda i, j: (i, j, 0)),
                  pl.BlockSpec((1, s, QCAT_W), lambda i, j: (i, 0, 0)),
                  pl.BlockSpec(wuv_exp.shape, lambda i, j: (0, 0, 0))],
        out_specs=pl.BlockSpec((1, tq, A_W), lambda i, j: (i, j, 0)),
        out_shape=jax.ShapeDtypeStruct((b, s, A_W), F32),
        scratch_shapes=[pltpu.VMEM((rows, LANES), F32), pltpu.VMEM((rows, QCAT_W), F32),
                        pltpu.VMEM((rows, tq), BF16), pltpu.VMEM((rows, tq), F32)],
        compiler_params=_cparams(("parallel", "arbitrary")),
        name="mla_attn",
    )(qcat, kcat, wuv_exp)


def _moba_attn_kernel(q_ref, k_ref, v_ref, bt_ref, o_ref, km_ref, kb_ref, vb_ref, qa_ref,
                      m_ref, acc_ref, p_ref, s_ref, *, nb, sub):
    blk = MOBA_BLOCK
    qi = pl.program_id(2)
    scale = HEAD_DIM ** -0.5 * LOG2E
    lane = lax.broadcasted_iota(jnp.int32, (1, LANES), 1)
    lane_f = lane.astype(F32)

    @pl.when(qi == 0)
    def _():
        k = k_ref[0]
        v = v_ref[0]
        km_ref[...] = jnp.zeros_like(km_ref)
        for n in range(nb):
            km_ref[n:n + 1, :] = jnp.mean(k[n * blk:(n + 1) * blk, :], axis=0, keepdims=True)
        row_blk = lax.broadcasted_iota(jnp.int32, k.shape, 0) // blk
        for hh in range(2):
            hmask = (lane // HEAD_DIM) == hh
            other = lane - (1 - hh) * HEAD_DIM
            kb_ref[hh] = jnp.where(hmask, k, jnp.where(other == row_blk, 1.0, 0.0)).astype(BF16)
            vb_ref[hh] = jnp.where(hmask, v, 1.0).astype(BF16)

    q = q_ref[0]
    km = km_ref[...]
    for hh in range(2):
        hmask = (lane // HEAD_DIM) == hh
        qh = jnp.where(hmask, q, 0.0)
        gate = lax.dot_general(qh, km, _NT, precision=lax.Precision.HIGHEST, preferred_element_type=F32)
        gate = jnp.where(lane < qi, gate, NEG_INF)
        picked = jnp.zeros_like(gate)
        for _ in range(MOBA_TOPK):
            mx = jnp.max(gate, axis=-1, keepdims=True)
            idx = jnp.min(jnp.where(gate == mx, lane_f, 1e9), axis=-1, keepdims=True)
            hit = lane_f == idx
            picked = jnp.where(hit, 1.0, picked)
            gate = jnp.where(hit, BELOW_NEG_INF, gate)
        attend = ((picked > 0.0) & (lane < qi)) | (lane == qi)
        pen = jnp.where(attend, 0.0, NEG_INF)
        if hh == 0:
            pen = pltpu.roll(pen, HEAD_DIM, axis=1)
        qa_ref[hh] = jnp.where(hmask, qh * scale, pen).astype(BF16)

    tiles = [(hh, r) for hh in range(2) for r in range(blk // sub)]

    def rows(hh, r):
        return slice(hh * blk + r * sub, hh * blk + (r + 1) * sub)

    def scores(hh, r, j):
        tile = jnp.minimum(qi - j, 2)
        kb = kb_ref[hh, pl.ds(pl.multiple_of(j * blk, blk), blk), :]
        s = lax.dot_general(qa_ref[hh, r * sub:(r + 1) * sub, :], kb, _NT, preferred_element_type=F32)
        return s + bt_ref[hh, tile, r * sub:(r + 1) * sub, :]

    def weighted(hh, p, j):
        return jnp.dot(p, vb_ref[hh, pl.ds(pl.multiple_of(j * blk, blk), blk), :], preferred_element_type=F32)

    def visit(t):
        return jnp.where(t == 0, qi, t - 1)

    j_first = visit(jnp.minimum(1, qi))
    for hh, r in tiles:
        rs = rows(hh, r)
        s_own = scores(hh, r, qi)
        m_own = jnp.broadcast_to(jnp.max(s_own, axis=-1, keepdims=True), (sub, LANES))
        p_ref[rs] = jnp.exp2(s_own - jnp.concatenate([m_own] * (blk // LANES), axis=1)).astype(BF16)
        m_ref[rs] = m_own
        acc_ref[rs] = jnp.zeros((sub, LANES), F32)
        s_ref[rs] = scores(hh, r, j_first)

    def step(t, carry):
        j_next = visit(jnp.minimum(t + 1, qi))
        j_prev = visit(jnp.maximum(t - 1, 0))
        for hh, r in tiles:
            rs = rows(hh, r)
            s_next = scores(hh, r, j_next)
            pv = weighted(hh, p_ref[rs], j_prev)
            s_cur = s_ref[rs]
            m_i = m_ref[rs]
            m_new = jnp.maximum(m_i, jnp.max(s_cur, axis=-1, keepdims=True))
            p_ref[rs] = jnp.exp2(s_cur - jnp.concatenate([m_new] * (blk // LANES), axis=1)).astype(BF16)
            acc_ref[rs] = jnp.exp2(m_i - m_new) * (acc_ref[rs] + pv)
            s_ref[rs] = s_next
            m_ref[rs] = m_new
        return carry

    lax.fori_loop(1, qi + 1, step, 0)
    j_last = visit(qi)
    for r in range(blk // sub):
        o = []
        for hh in range(2):
            rs = rows(hh, r)
            acc = acc_ref[rs] + weighted(hh, p_ref[rs], j_last)
            o.append(acc / pltpu.roll(acc, HEAD_DIM, axis=1))
        o_ref[0, r * sub:(r + 1) * sub, :] = jnp.where(lane < HEAD_DIM, o[0], o[1])


def _moba_attn(q, k, v, btab):
    b, s, _ = q.shape
    blk = MOBA_BLOCK
    nb = s // blk
    hp = A_W // LANES
    assert nb <= HEAD_DIM
    sub = 256
    return pl.pallas_call(
        functools.partial(_moba_attn_kernel, nb=nb, sub=sub),
        grid=(b, hp, nb),
        in_specs=[pl.BlockSpec((1, blk, LANES), lambda i, h, j: (i, j, h)),
                  pl.BlockSpec((1, s, LANES), lambda i, h, j: (i, 0, h)),
                  pl.BlockSpec((1, s, LANES), lambda i, h, j: (i, 0, h)),
                  pl.BlockSpec((2, 3, blk, blk), lambda i, h, j: (h, 0, 0, 0))],
        out_specs=pl.BlockSpec((1, blk, LANES), lambda i, h, j: (i, j, h)),
        out_shape=jax.ShapeDtypeStruct((b, s, A_W), F32),
        scratch_shapes=[pltpu.VMEM((LANES, LANES), F32),
                        pltpu.VMEM((2, s, LANES), BF16),
                        pltpu.VMEM((2, s, LANES), BF16),
                        pltpu.VMEM((2, blk, LANES), BF16),
                        pltpu.VMEM((2 * blk, LANES), F32), pltpu.VMEM((2 * blk, LANES), F32),
                        pltpu.VMEM((2 * blk, blk), BF16), pltpu.VMEM((2 * blk, blk), F32)],
        compiler_params=_cparams(("parallel", "parallel", "arbitrary")),
        name="moba_attn",
    )(q, k, v, btab)


def _ret_kernel(r_ref, cos_ref, sin_ref, intra_ref, into_ref, outof_ref, carry_ref, bd_ref, gm_ref,
                gn_ref, o_ref, st_ref, s_scr):
    ci = pl.program_id(1)

    @pl.when(ci == 0)
    def _():
        s_scr[...] = jnp.zeros_like(s_scr)

    r = r_ref[0]
    cos = cos_ref[...]
    sin = sin_ref[...]

    def rope(x):
        return jnp.concatenate(
            [_rope_slab(x[:, s * LANES:(s + 1) * LANES], cos[:, s * LANES:(s + 1) * LANES],
                        sin[:, s * LANES:(s + 1) * LANES], HEAD_DIM) for s in range(R_W // LANES)], axis=1)

    q = rope(r[:, 0:R_W])
    k = rope(r[:, R_W:2 * R_W]) * (HEAD_DIM ** -0.5)
    v = r[:, 2 * R_W:3 * R_W]
    gate = r[:, 3 * R_W:]
    lane_head = lax.broadcasted_iota(jnp.int32, (1, R_W), 1) // HEAD_DIM
    kb = k.astype(BF16)
    state = s_scr[...]
    o = jnp.dot((q * into_ref[...]).astype(BF16), state.astype(BF16), preferred_element_type=F32)
    for h in range(H_B):
        qh = jnp.where(lane_head == h, q, 0.0).astype(BF16)
        a = lax.dot_general(qh, kb, _NT, preferred_element_type=F32) * intra_ref[h]
        vh = jnp.where(lane_head == h, v, 0.0).astype(BF16)
        o = o + jnp.dot(a.astype(BF16), vh, preferred_element_type=F32)
    kv = lax.dot_general((k * outof_ref[...]).astype(BF16), v.astype(BF16), _TN, preferred_element_type=F32)
    new_state = state * carry_ref[...] + kv * bd_ref[...]
    s_scr[...] = new_state
    st_ref[0] = new_state
    sq = o * o
    sq_hi = sq.astype(BF16)
    sq_lo = (sq - sq_hi.astype(F32)).astype(BF16)
    ms = (jnp.dot(sq_hi, gm_ref[...], preferred_element_type=F32)
          + jnp.dot(sq_lo, gm_ref[...], preferred_element_type=F32))
    ob = o * lax.rsqrt(ms + NORM_EPS) * gn_ref[...]
    o_ref[0] = ob * (gate * jax.nn.sigmoid(gate))


def _retention_tables(c):
    log_g = jnp.log(1.0 - 2.0 ** (-5.0 - jnp.arange(H_B, dtype=F32)))
    i = jnp.arange(c, dtype=F32)
    diff = i[:, None] - i[None, :]
    intra = jnp.where(diff >= 0, jnp.exp(log_g[:, None, None] * jnp.maximum(diff, 0.0)), 0.0)
    lg_lane = jnp.repeat(log_g, HEAD_DIM)
    into = jnp.exp(lg_lane[None, :] * (i[:, None] + 1.0))
    outof = jnp.exp(lg_lane[None, :] * (c - 1.0 - i[:, None]))
    carry = jnp.exp(lg_lane * c)[:, None]
    head = jnp.arange(R_W) // HEAD_DIM
    bd = (head[:, None] == head[None, :]).astype(F32)
    return intra, into, outof, carry, bd, (bd / HEAD_DIM).astype(BF16)


def _retention(r, cos, sin, gn):
    b, s, _ = r.shape
    c = min(s, 256)
    intra, into, outof, carry, bd, gm = _retention_tables(c)
    full = lambda a: pl.BlockSpec(a.shape, lambda i, j: (0,) * a.ndim)
    tab = pl.BlockSpec((c, R_W), lambda i, j: (j, 0))
    return pl.pallas_call(
        _ret_kernel,
        grid=(b, s // c),
        in_specs=[pl.BlockSpec((1, c, 4 * R_W), lambda i, j: (i, j, 0)), tab, tab,
                  full(intra), full(into), full(outof), full(carry), full(bd), full(gm), full(gn)],
        out_specs=[pl.BlockSpec((1, c, R_W), lambda i, j: (i, j, 0)),
                   pl.BlockSpec((1, R_W, R_W), lambda i, j: (i, 0, 0))],
        out_shape=[jax.ShapeDtypeStruct((b, s, R_W), F32), jax.ShapeDtypeStruct((b, R_W, R_W), F32)],
        scratch_shapes=[pltpu.VMEM((R_W, R_W), F32)],
        compiler_params=_cparams(("parallel", "arbitrary")),
        name="retention",
    )(r, cos, sin, intra, into, outof, carry, bd, gm, gn)


def _ret_step_kernel(q_ref, k_ref, v_ref, g_ref, st_ref, cos_ref, sin_ref, dec_ref, gn_ref, o_ref, ns_ref):
    half = HEAD_DIM // 2
    cos = cos_ref[...]
    sin = sin_ref[...]

    def rope_col(x):
        swapped = jnp.concatenate([x[:, half:, :], x[:, :half, :]], axis=1)
        return x * cos + swapped * sin

    q = rope_col(q_ref[0])
    k = rope_col(k_ref[0]) * (HEAD_DIM ** -0.5)
    v = v_ref[0]
    new_state = st_ref[0] * dec_ref[...] + k * v
    ns_ref[0] = new_state
    o = jnp.sum(q * new_state, axis=1, keepdims=True)
    ob = o * lax.rsqrt(jnp.mean(o * o, axis=-1, keepdims=True) + NORM_EPS) * gn_ref[...]
    gate = g_ref[0]
    o_ref[0] = ob * (gate * jax.nn.sigmoid(gate))


def _retention_step(r, state, pos, gn):
    b = r.shape[0]
    col = lambda x: x.reshape(b, H_B, HEAD_DIM, 1)
    rowv = lambda x: x.reshape(b, H_B, 1, HEAD_DIM)
    half = HEAD_DIM // 2
    inv = RET_ROPE_BASE ** (-jnp.arange(half, dtype=F32) / half)
    ang = jnp.asarray(pos, F32) * inv
    cos = jnp.concatenate([jnp.cos(ang), jnp.cos(ang)]).reshape(1, HEAD_DIM, 1)
    sin = jnp.concatenate([-jnp.sin(ang), jnp.sin(ang)]).reshape(1, HEAD_DIM, 1)
    log_g = jnp.log(1.0 - 2.0 ** (-5.0 - jnp.arange(H_B, dtype=F32)))
    dec = jnp.exp(log_g * 1.0).reshape(H_B, 1, 1)
    gn4 = gn.reshape(H_B, 1, HEAD_DIM)
    cspec = pl.BlockSpec((1, H_B, HEAD_DIM, 1), lambda i: (i, 0, 0, 0))
    rspec = pl.BlockSpec((1, H_B, 1, HEAD_DIM), lambda i: (i, 0, 0, 0))
    sspec = pl.BlockSpec((1, H_B, HEAD_DIM, HEAD_DIM), lambda i: (i, 0, 0, 0))
    full = lambda a: pl.BlockSpec(a.shape, lambda i: (0,) * a.ndim)
    o, ns = pl.pallas_call(
        _ret_step_kernel,
        grid=(b,),
        in_specs=[cspec, cspec, rspec, rspec, sspec, full(cos), full(sin), full(dec), full(gn4)],
        out_specs=[rspec, sspec],
        out_shape=[jax.ShapeDtypeStruct((b, H_B, 1, HEAD_DIM), F32),
                   jax.ShapeDtypeStruct((b, H_B, HEAD_DIM, HEAD_DIM), F32)],
        compiler_params=_cparams(("parallel",)),
        name="retention_step",
    )(col(r[:, 0:R_W]), col(r[:, R_W:2 * R_W]), rowv(r[:, 2 * R_W:3 * R_W]), rowv(r[:, 3 * R_W:]),
      state, cos, sin, dec, gn4)
    return o.reshape(b, R_W), ns


def _out_proj_kernel(x_ref, a_ref, b_ref, c_ref, wa_ref, wb_ref, wc_ref, y_ref):
    y = x_ref[...]
    y = y + jnp.dot(a_ref[...].astype(BF16), wa_ref[...], preferred_element_type=F32)
    y = y + jnp.dot(b_ref[...].astype(BF16), wb_ref[...], preferred_element_type=F32)
    y = y + jnp.dot(c_ref[...].astype(BF16), wc_ref[...], preferred_element_type=F32)
    y_ref[...] = y


def _out_proj(x, oa, ob, oc, wa, wb, wc):
    m = x.shape[0]
    tm = min(m, 512)
    row = lambda w_: pl.BlockSpec((tm, w_), lambda i: (i, 0))
    full = lambda a: pl.BlockSpec(a.shape, lambda i: (0,) * a.ndim)
    return pl.pallas_call(
        _out_proj_kernel,
        grid=(m // tm,),
        in_specs=[row(D_MODEL), row(A_W), row(R_W), row(A_W), full(wa), full(wb), full(wc)],
        out_specs=row(D_MODEL),
        out_shape=jax.ShapeDtypeStruct((m, D_MODEL), F32),
        compiler_params=_cparams(("parallel",)),
        name="out_proj",
    )(x, oa, ob, oc, wa, wb, wc)


_HALO = 16


def _ffn_kernel(pt_ref, xm_ref, xh_ref, st_ref, g2_ref, wup_ref, cw_ref, cb_ref, wdn_ref, fg_ref, kc_ref,
                y_ref, cn_ref, km_ref, hs_ref, u_ref, act_ref, kbuf, ksem, *, tm, nch, final, layer, sps):
    i = pl.program_id(1)
    step = pl.program_id(0) * pl.num_programs(1) + i
    nsteps = pl.num_programs(0) * pl.num_programs(1)
    nring, pps, page = kbuf.shape[0], kbuf.shape[1], kbuf.shape[-1]
    ppb = MOBA_BLOCK // page
    bps = pps // ppb
    lane3 = lax.broadcasted_iota(jnp.int32, (1, 1, LANES), 2)

    def kcopies(seq, sl):
        ring = (seq * sps + sl) % nring
        return [pltpu.make_async_copy(kc_ref.at[layer, pt_ref[seq, sl * pps + p]], kbuf.at[ring, p], ksem.at[ring])
                for p in range(pps)]

    def kstart(seq, sl):
        for p, cp in enumerate(kcopies(seq, sl)):
            cp.start(priority=p % 2)

    @pl.when(step == 0)
    def _():
        for sl in range(nring - 1):
            kstart(0, sl)

    km_ref[0] = jnp.zeros(km_ref.shape[1:], F32)

    def stream(sl):
        ahead = sl + nring - 1
        if ahead < sps:
            kstart(step, ahead)
        else:
            @pl.when(step + 1 < nsteps)
            def _():
                kstart(step + 1, ahead - sps)
        for cp in kcopies(step, sl):
            cp.wait()
        ring = (step * sps + sl) % nring
        km = km_ref[0]
        for n in range(bps):
            tot = kbuf[ring, n * ppb]
            for p in range(1, ppb):
                tot = tot + kbuf[ring, n * ppb + p]
            mean = jnp.sum(tot, axis=-1, keepdims=True) * (1.0 / MOBA_BLOCK)
            km = jnp.where(lane3 == sl * bps + n, mean, km)
        km_ref[0] = km

    g2 = g2_ref[...]
    hs_ref[0:_HALO, :] = _rms(xh_ref[0], g2).astype(BF16)
    hs_ref[_HALO:, :] = _rms(xm_ref[0], g2).astype(BF16)
    lo = _HALO - (CONV_W - 1)
    rc = min(tm, 128)
    row8 = lax.broadcasted_iota(jnp.int32, (8, FFN_CH), 0)
    from_state = (row8 >= 8 - (CONV_W - 1)) & (i == 0)

    def up(c, slot):
        for part in range(2):
            u_ref[slot, part] = jnp.dot(hs_ref[...], wup_ref[part, c], preferred_element_type=F32)
            edge = u_ref[slot, part, _HALO - 8:_HALO, :]
            u_ref[slot, part, _HALO - 8:_HALO, :] = jnp.where(from_state, st_ref[0, part, c], edge)

    def gate(c, slot):
        for r0 in range(0, tm, rc):
            convs = []
            for part in range(2):
                w = cw_ref[part, c]
                conv = cb_ref[part, c]
                for t in range(CONV_W):
                    conv = conv + w[t:t + 1, :] * u_ref[slot, part, lo + t + r0:lo + t + r0 + rc, :]
                convs.append(conv)
            act_ref[r0:r0 + rc, c * FFN_CH:(c + 1) * FFN_CH] = (
                (convs[0] * jax.nn.sigmoid(convs[0])) * convs[1]).astype(BF16)
        for part in range(2):
            cn_ref[0, part, c] = u_ref[slot, part, _HALO + tm - (CONV_W - 1):_HALO + tm, :]

    up(0, 0)
    for c in range(nch):
        if c < sps:
            stream(c)
        if c + 1 < nch:
            up(c + 1, (c + 1) & 1)
        gate(c, c & 1)
    y = xm_ref[0] + jnp.dot(act_ref[...], wdn_ref[...], preferred_element_type=F32)
    y_ref[0] = _rms(y, fg_ref[...]) if final else y


def _ffn(x, state5, g2, wup, cw, cb, wdn, fg, final, page_table, kcache_t, layer):
    b, s, _ = x.shape
    nch = wdn.shape[0]
    tm = min(s, 512)
    nt = s // tm
    hb = tm // _HALO
    nseq, npages = page_table.shape
    page = kcache_t.shape[-1]
    nbp = npages * page // MOBA_BLOCK
    sps = max(d for d in range(1, nch + 1) if nbp % d == 0)
    pps = npages // sps
    assert nseq == b * nt and nbp <= LANES and sps >= DMA_SLOTS - 1 and pps % (MOBA_BLOCK // page) == 0
    full = lambda a: pl.BlockSpec(a.shape, lambda i, j, pt: (0,) * a.ndim)
    state8 = jnp.pad(state5, ((0, 0), (0, 0), (0, 0), (8 - (CONV_W - 1), 0), (0, 0)))
    st_spec = lambda a: pl.BlockSpec((1,) + a.shape[1:], lambda i, j, pt: (i, 0, 0, 0, 0))
    wdn = wdn.reshape(nch * FFN_CH, D_MODEL)
    km_shape = (nseq, H_A, HEAD_DIM, LANES)
    grid_spec = pltpu.PrefetchScalarGridSpec(
        num_scalar_prefetch=1,
        grid=(b, nt),
        in_specs=[pl.BlockSpec((1, tm, D_MODEL), lambda i, j, pt: (i, j, 0)),
                  pl.BlockSpec((1, _HALO, D_MODEL), lambda i, j, pt: (i, jnp.maximum(j * hb - 1, 0), 0)),
                  st_spec(state8), full(g2), full(wup), full(cw), full(cb), full(wdn), full(fg),
                  pl.BlockSpec(memory_space=pl.ANY)],
        out_specs=[pl.BlockSpec((1, tm, D_MODEL), lambda i, j, pt: (i, j, 0)), st_spec(state5),
                   pl.BlockSpec((1,) + km_shape[1:], lambda i, j, pt: (i * nt + j, 0, 0, 0))],
        scratch_shapes=[pltpu.VMEM((tm + _HALO, D_MODEL), BF16),
                        pltpu.VMEM((2, 2, tm + _HALO, FFN_CH), F32),
                        pltpu.VMEM((tm, nch * FFN_CH), BF16),
                        pltpu.VMEM((DMA_SLOTS, pps, H_A, HEAD_DIM, page), F32),
                        pltpu.SemaphoreType.DMA((DMA_SLOTS,))],
    )
    return pl.pallas_call(
        functools.partial(_ffn_kernel, tm=tm, nch=nch, final=final, layer=layer, sps=sps),
        grid_spec=grid_spec,
        out_shape=[jax.ShapeDtypeStruct((b, s, D_MODEL), F32), jax.ShapeDtypeStruct(state5.shape, F32),
                   jax.ShapeDtypeStruct(km_shape, F32)],
        compiler_params=_cparams(("arbitrary", "arbitrary")),
        name="conv_ffn",
    )(page_table, x, x, state8, g2, wup, cw, cb, wdn, fg, kcache_t)


def _ffn_step_kernel(x_ref, st_ref, g2_ref, wup_ref, cw_ref, cb_ref, wdn_ref, fg_ref,
                     y_ref, cn_ref, hs_ref, *, final):
    c = pl.program_id(0)

    @pl.when(c == 0)
    def _():
        hs_ref[...] = _rms(x_ref[...], g2_ref[...]).astype(BF16)
        y_ref[...] = x_ref[...]

    convs = []
    for part in range(2):
        u = jnp.dot(hs_ref[...], wup_ref[part, 0], preferred_element_type=F32)
        w = cw_ref[part, 0]
        s0 = st_ref[0, part, 0]
        s1 = st_ref[0, part, 1]
        convs.append(cb_ref[part, 0] + w[0:1, :] * s0 + w[1:2, :] * s1 + w[2:3, :] * u)
        cn_ref[0, part, 0] = s1
        cn_ref[0, part, 1] = u
    act = (convs[0] * jax.nn.sigmoid(convs[0])) * convs[1]
    y_ref[...] += jnp.dot(act.astype(BF16), wdn_ref[0], preferred_element_type=F32)
    if final:
        @pl.when(c == pl.num_programs(0) - 1)
        def _():
            y_ref[...] = _rms(y_ref[...], fg_ref[...])


def _ffn_step(x, state5, g2, wup, cw, cb, wdn, fg, final):
    b = x.shape[0]
    nch = wdn.shape[0]
    full = lambda a: pl.BlockSpec(a.shape, lambda c: (0,) * a.ndim)
    st_spec = pl.BlockSpec((1,) + state5.shape[1:], lambda c: (c, 0, 0, 0, 0))
    return pl.pallas_call(
        functools.partial(_ffn_step_kernel, final=final),
        grid=(nch,),
        in_specs=[full(x), st_spec, full(g2),
                  pl.BlockSpec((2, 1, D_MODEL, FFN_CH), lambda c: (0, c, 0, 0)),
                  pl.BlockSpec((2, 1, CONV_W, FFN_CH), lambda c: (0, c, 0, 0)),
                  pl.BlockSpec((2, 1, 1, FFN_CH), lambda c: (0, c, 0, 0)),
                  pl.BlockSpec((1, FFN_CH, D_MODEL), lambda c: (c, 0, 0)),
                  full(fg)],
        out_specs=[full(x), st_spec],
        out_shape=[jax.ShapeDtypeStruct((b, D_MODEL), F32), jax.ShapeDtypeStruct(state5.shape, F32)],
        scratch_shapes=[pltpu.VMEM((b, D_MODEL), BF16)],
        compiler_params=_cparams(("arbitrary",)),
        name="conv_ffn_step",
    )(x, state5, g2, wup, cw, cb, wdn, fg)


def _moba_sel_kernel(q_ref, km_ref, sel_ref, *, nbp):
    lane = lax.broadcasted_iota(jnp.int32, (1, 1, LANES), 2)
    gate = jnp.sum(km_ref[0] * q_ref[0], axis=1, keepdims=True)
    lane_f = lane.astype(F32)
    gate = jnp.where(lane < nbp, gate, BELOW_NEG_INF)
    out = jnp.zeros_like(gate)
    for t in range(MOBA_TOPK):
        mx = jnp.max(gate, axis=-1, keepdims=True)
        idx = jnp.min(jnp.where(gate == mx, lane_f, 1e9), axis=-1, keepdims=True)
        out = jnp.where(lane == t, idx, out)
        gate = jnp.where(lane_f == idx, BELOW_NEG_INF, gate)
    sel_ref[0] = out.astype(jnp.int32)


def _moba_select(q, kmean, nbp):
    b = q.shape[0]
    sel = pl.pallas_call(
        functools.partial(_moba_sel_kernel, nbp=nbp),
        grid=(b,),
        in_specs=[pl.BlockSpec((1, H_A, HEAD_DIM, 1), lambda i: (i, 0, 0, 0)),
                  pl.BlockSpec((1, H_A, HEAD_DIM, LANES), lambda i: (i, 0, 0, 0))],
        out_specs=pl.BlockSpec((1, H_A, 1, LANES), lambda i: (i, 0, 0, 0)),
        out_shape=jax.ShapeDtypeStruct((b, H_A, 1, LANES), jnp.int32),
        compiler_params=_cparams(("parallel",)),
        name="moba_select",
    )(q.reshape(b, H_A, HEAD_DIM, 1), kmean)
    return sel[:, :, 0, :MOBA_TOPK]


def _moba_dec_kernel(pt_ref, sel_ref, q_ref, kn_ref, vn_ref, bl_ref, bs_ref, kc_ref, vc_ref, o_ref,
                     kbuf, vbuf, sem, *, layer, nbp, ppb):
    b = pl.program_id(0)
    nseq = pl.num_programs(0)
    scale = HEAD_DIM ** -0.5
    npg = MOBA_TOPK * ppb

    def copies(bb, slot):
        out = []
        for h in range(H_A):
            for t in range(MOBA_TOPK):
                n = sel_ref[bb, h * MOBA_TOPK + t]
                for p in range(ppb):
                    pg = pt_ref[bb, n * ppb + p]
                    out.append(pltpu.make_async_copy(kc_ref.at[layer, pg, h], kbuf.at[slot, h, t * ppb + p],
                                                     sem.at[0, slot]))
                    out.append(pltpu.make_async_copy(vc_ref.at[layer, pg, h], vbuf.at[slot, h, t * ppb + p],
                                                     sem.at[1, slot]))
        return out

    @pl.when(b == 0)
    def _():
        for cp in copies(0, 0):
            cp.start()

    slot = b & 1

    @pl.when(b + 1 < nseq)
    def _():
        for cp in copies(b + 1, 1 - slot):
            cp.start()

    for cp in copies(b, slot):
        cp.wait()

    for h in range(H_A):
        qh = q_ref[0, h] * scale
        far = bs_ref[h, 0]
        rows = []
        for t in range(MOBA_TOPK):
            is_last = sel_ref[b, h * MOBA_TOPK + t] == nbp - 1
            for p in range(ppb):
                s = jnp.sum(kbuf[slot, h, t * ppb + p] * qh, axis=0, keepdims=True)
                rows.append(s + jnp.where(is_last, bl_ref[h, p:p + 1, :], far))
        s_sel = jnp.concatenate(rows, axis=0)
        s_new = jnp.sum(kn_ref[0, h] * qh, axis=0, keepdims=True) + bs_ref[h, 1]
        mx = jnp.maximum(jnp.max(jnp.max(s_sel, axis=1, keepdims=True), axis=0, keepdims=True), s_new)
        p_sel = jnp.exp(s_sel - mx)
        p_new = jnp.exp(s_new - mx)
        denom = jnp.sum(jnp.sum(p_sel, axis=1, keepdims=True), axis=0, keepdims=True) + p_new
        pv = vbuf[slot, h, 0] * p_sel[0:1, :]
        for g in range(1, npg):
            pv = pv + vbuf[slot, h, g] * p_sel[g:g + 1, :]
        o = jnp.sum(pv, axis=1, keepdims=True) + p_new * vn_ref[0, h]
        o_ref[0, h] = o / denom


def _t5_bucket(dist):
    n = jnp.maximum(dist, 0)
    max_exact = T5_BUCKETS // 2
    nf = jnp.maximum(n, 1).astype(F32)
    large = max_exact + jnp.floor(jnp.log(nf / max_exact) / math.log(T5_MAX_DIST / max_exact)
                                  * (T5_BUCKETS - max_exact))
    large = jnp.minimum(large, float(T5_BUCKETS - 1))
    return jnp.where(n < max_exact, n.astype(F32), large)


def _moba_decode(page_table, sel, q, k_new, v_new, kcache_t, vcache_t, layer, rel_bias, past_len):
    b, npages = page_table.shape
    page = kcache_t.shape[-1]
    ppb = MOBA_BLOCK // page
    nbp = past_len // MOBA_BLOCK
    off = jnp.arange(MOBA_BLOCK)
    dist_last = past_len - ((nbp - 1) * MOBA_BLOCK + off)
    b_last = _bias_lookup(rel_bias, _t5_bucket(dist_last)).reshape(H_A, ppb, page)
    dist_far = jnp.asarray([past_len - (nbp - 1) * MOBA_BLOCK + 1, 0])
    b_sc = _bias_lookup(rel_bias, _t5_bucket(dist_far))
    sel_flat = sel.reshape(b, H_A * MOBA_TOPK)
    col = lambda x: x.reshape(b, H_A, HEAD_DIM, 1)
    cspec = pl.BlockSpec((1, H_A, HEAD_DIM, 1), lambda i, pt, sl: (i, 0, 0, 0))
    grid_spec = pltpu.PrefetchScalarGridSpec(
        num_scalar_prefetch=2,
        grid=(b,),
        in_specs=[cspec, cspec, cspec,
                  pl.BlockSpec(b_last.shape, lambda i, pt, sl: (0, 0, 0)),
                  pl.BlockSpec(memory_space=pltpu.SMEM),
                  pl.BlockSpec(memory_space=pl.ANY),
                  pl.BlockSpec(memory_space=pl.ANY)],
        out_specs=cspec,
        scratch_shapes=[pltpu.VMEM((2, H_A, MOBA_TOPK * ppb, HEAD_DIM, page), F32),
                        pltpu.VMEM((2, H_A, MOBA_TOPK * ppb, HEAD_DIM, page), F32),
                        pltpu.SemaphoreType.DMA((2, 2))],
    )
    o = pl.pallas_call(
        functools.partial(_moba_dec_kernel, layer=layer, nbp=nbp, ppb=ppb),
        grid_spec=grid_spec,
        out_shape=jax.ShapeDtypeStruct((b, H_A, HEAD_DIM, 1), F32),
        compiler_params=_cparams(("arbitrary",)),
        name="moba_decode",
    )(page_table, sel_flat, col(q), col(k_new), col(v_new), b_last, b_sc, kcache_t, vcache_t)
    return o.reshape(b, A_W)


def _mla_dec_kernel(pt_ref, q_ref, kn_ref, mc_ref, o_ref, buf, sem, *, layer, pc):
    b = pl.program_id(0)
    nseq = pl.num_programs(0)
    npages = pt_ref.shape[1]
    nchunks = npages // pc
    q = q_ref[0]
    q_lat = q[:, 0:KV_LORA]
    q_rope = q[:, KV_LORA:KV_LORA + D_ROPE]

    nslots = buf.shape[0]
    total = nseq * nchunks

    def copies(g):
        bb = g // nchunks
        c = g % nchunks
        slot = g % nslots
        return [pltpu.make_async_copy(mc_ref.at[layer, pt_ref[bb, c * pc + p]], buf.at[slot, p], sem.at[slot])
                for p in range(pc)]

    def start(g):
        for p, cp in enumerate(copies(g)):
            cp.start(priority=p % 2)

    @pl.when(b == 0)
    def _():
        for g in range(nslots - 1):
            start(g)

    def keys(g):
        slot = g % nslots
        return jnp.concatenate([buf[slot, p] for p in range(pc)], axis=1).astype(BF16)

    def scores(kv):
        return (jnp.dot(q_lat, kv[0:KV_LORA], preferred_element_type=F32)
                + jnp.dot(q_rope, kv[KV_LORA:], preferred_element_type=F32))

    def chunk(c, carry):
        m_i, l_i, acc, s = carry
        g = b * nchunks + c

        @pl.when(g + nslots - 1 < total)
        def _():
            start(g + nslots - 1)

        @pl.when(c + 1 < nchunks)
        def _():
            for cp in copies(g + 1):
                cp.wait()

        s_next = scores(keys(b * nchunks + jnp.minimum(c + 1, nchunks - 1)))
        m_new = jnp.maximum(m_i, jnp.max(s, axis=-1, keepdims=True))
        alpha = jnp.exp2(m_i - m_new)
        p = jnp.exp2(s - m_new)
        l_new = alpha * l_i + jnp.sum(p, axis=-1, keepdims=True)
        acc_new = alpha * acc + lax.dot_general(p.astype(BF16), keys(g)[0:KV_LORA], _NT,
                                                preferred_element_type=F32)
        return m_new, l_new, acc_new, s_next

    for cp in copies(b * nchunks):
        cp.wait()
    init = (jnp.full((8, 1), NEG_INF, F32), jnp.zeros((8, 1), F32), jnp.zeros((8, KV_LORA), F32),
            scores(keys(b * nchunks)))
    m_i, l_i, acc, _ = lax.fori_loop(0, nchunks, chunk, init)
    kn = kn_ref[0].astype(F32)
    s_new = jnp.sum(q.astype(F32) * kn, axis=-1, keepdims=True)
    m_new = jnp.maximum(m_i, s_new)
    alpha = jnp.exp2(m_i - m_new)
    p_new = jnp.exp2(s_new - m_new)
    l_fin = alpha * l_i + p_new
    acc_fin = alpha * acc + p_new * kn[:, 0:KV_LORA]
    o_ref[0] = acc_fin / l_fin


def _mla_decode(page_table, qcat, kcat_new, mcache_t, layer):
    b, npages = page_table.shape
    page = mcache_t.shape[-1]
    pc = min(npages, 16)
    q8 = jnp.pad(qcat.reshape(b, H_C, QCAT_W), ((0, 0), (0, 8 - H_C), (0, 0)))
    grid_spec = pltpu.PrefetchScalarGridSpec(
        num_scalar_prefetch=1,
        grid=(b,),
        in_specs=[pl.BlockSpec((1, 8, QCAT_W), lambda i, pt: (i, 0, 0)),
                  pl.BlockSpec((1, 1, QCAT_W), lambda i, pt: (i, 0, 0)),
                  pl.BlockSpec(memory_space=pl.ANY)],
        out_specs=pl.BlockSpec((1, 8, KV_LORA), lambda i, pt: (i, 0, 0)),
        scratch_shapes=[pltpu.VMEM((DMA_SLOTS, pc, KV_LORA + D_ROPE, page), F32),
                        pltpu.SemaphoreType.DMA((DMA_SLOTS,))],
    )
    o_lat = pl.pallas_call(
        functools.partial(_mla_dec_kernel, layer=layer, pc=pc),
        grid_spec=grid_spec,
        out_shape=jax.ShapeDtypeStruct((b, 8, KV_LORA), F32),
        compiler_params=_cparams(("arbitrary",)),
        name="mla_decode",
    )(page_table, q8, kcat_new.reshape(b, 1, QCAT_W), mcache_t)
    return o_lat[:, :H_C, :].reshape(b, H_C * KV_LORA)


def _matmul_kernel(a_ref, w_ref, o_ref):
    o_ref[...] = jnp.dot(a_ref[...].astype(BF16), w_ref[...], preferred_element_type=F32)


def _matmul(a, w):
    full = lambda x: pl.BlockSpec(x.shape, lambda: (0,) * x.ndim)
    return pl.pallas_call(
        _matmul_kernel,
        in_specs=[full(a), full(w)],
        out_specs=pl.BlockSpec((a.shape[0], w.shape[1]), lambda: (0, 0)),
        out_shape=jax.ShapeDtypeStruct((a.shape[0], w.shape[1]), F32),
        name="latent_out",
    )(a, w)


def _rope_tables(pos, base, group, width):
    half = group // 2
    inv = base ** (-jnp.arange(half, dtype=F32) / half)
    ang = pos.astype(F32)[:, None] * inv[None, :]
    cos = jnp.concatenate([jnp.cos(ang), jnp.cos(ang)], axis=1)
    sin = jnp.concatenate([-jnp.sin(ang), jnp.sin(ang)], axis=1)
    reps = width // group
    return jnp.tile(cos, (1, reps)), jnp.tile(sin, (1, reps))


def _bias_lookup(rel_bias, buckets):
    onehot = (buckets[..., None] == jnp.arange(T5_BUCKETS, dtype=F32)).astype(F32)
    out = jnp.einsum("...k,kh->...h", onehot, rel_bias.astype(F32), precision=lax.Precision.HIGHEST)
    return jnp.moveaxis(out, -1, 0)


def _moba_bias_tiles(rel_bias):
    r = jnp.arange(MOBA_BLOCK)
    d0 = r[:, None] - r[None, :]
    own = jnp.where((d0 >= 0)[None], _bias_lookup(rel_bias, _t5_bucket(d0)), NEG_INF)
    prev = _bias_lookup(rel_bias, _t5_bucket(d0 + MOBA_BLOCK))
    far = jnp.broadcast_to(_bias_lookup(rel_bias, _t5_bucket(jnp.asarray([MOBA_BLOCK + 1])))[:, :, None], own.shape)
    return jnp.stack([own, prev, far], axis=1) * LOG2E


def _layer_weights(w_in, w_uq, w_uk, w_uv, w_o, w_up, conv_w, conv_b, w_down):
    d_ff = w_down.shape[0]
    nch = d_ff // FFN_CH
    w = {}
    w["in"] = jnp.pad(w_in, ((0, 0), (0, N_IN_PAD - N_IN))).astype(BF16)
    uq = w_uq.reshape(Q_LORA, H_C, D_NOPE + D_ROPE)
    w["uq_nope"] = uq[:, :, :D_NOPE].reshape(Q_LORA, H_C * D_NOPE).astype(BF16)
    w["uq_rope"] = jnp.pad(uq[:, :, D_NOPE:].reshape(Q_LORA, H_C * D_ROPE),
                           ((0, 0), (0, 2 * LANES - H_C * D_ROPE))).astype(BF16)
    eye = jnp.eye(H_C, dtype=F32)
    w["uk_bd"] = jnp.einsum("rhd,hg->hdgr", w_uk, eye).reshape(H_C * D_NOPE, H_C * KV_LORA).astype(BF16)
    w["uv_exp"] = jnp.einsum("rhd,hg->hrgd", w_uv, eye).reshape(H_C, KV_LORA, H_C * HEAD_DIM).astype(BF16)
    w["o_a"] = w_o[0:A_W].astype(BF16)
    w["o_b"] = w_o[A_W:A_W + R_W].astype(BF16)
    w["o_c"] = w_o[A_W + R_W:].astype(BF16)
    w["up"] = w_up.reshape(D_MODEL, 2, nch, FFN_CH).transpose(1, 2, 0, 3).astype(BF16)
    w["conv_w"] = conv_w.reshape(CONV_W, 2, nch, FFN_CH).transpose(1, 2, 0, 3)
    w["conv_b"] = conv_b.reshape(2, nch, 1, FFN_CH)
    w["down"] = w_down.reshape(nch, FFN_CH, D_MODEL).astype(BF16)
    return w


def _state_to_chunks(state, nch):
    b = state.shape[0]
    return state.reshape(b, CONV_W - 1, 2, nch, FFN_CH).transpose(0, 2, 3, 1, 4)


def _chunks_to_state(c5):
    b, _, nch, rows, ch = c5.shape
    return c5.transpose(0, 3, 1, 2, 4).reshape(b, rows, 2 * nch * ch)


def _ret_state_blocks(sfull):
    b = sfull.shape[0]
    s6 = sfull.reshape(b, H_B, HEAD_DIM, H_B, HEAD_DIM)
    return jnp.stack([s6[:, h, :, h, :] for h in range(H_B)], axis=1)


def _prompt_layer(x, w, g1, gq, gkv, gret, g2, fg, btab, tabs, final, page_table, kcache_t, layer):
    b, s, _ = x.shape
    m = b * s
    x2 = x.reshape(m, D_MODEL)
    a_q, a_k, a_v, r, mla_new, kcat, qcat = _in_proj(x2, g1, w["in"], gq, gkv, w["uq_nope"], w["uq_rope"], w["uk_bd"],
                                                     tabs["mla_cos"], tabs["mla_sin"], s)
    o_a = _moba_attn(a_q.reshape(b, s, A_W), a_k.reshape(b, s, A_W), a_v.reshape(b, s, A_W), btab)
    o_b, sfull = _retention(r.reshape(b, s, 4 * R_W), tabs["ret_cos"], tabs["ret_sin"], gret)
    o_c = _mla_attn(qcat.reshape(b, s, H_C * QCAT_W), kcat.reshape(b, s, QCAT_W), w["uv_exp"])
    x1 = _out_proj(x2, o_a.reshape(m, A_W), o_b.reshape(m, R_W), o_c.reshape(m, A_W), w["o_a"], w["o_b"], w["o_c"])
    nch = w["down"].shape[0]
    st0 = jnp.zeros((b, 2, nch, CONV_W - 1, FFN_CH), F32)
    y, cn, kmean = _ffn(x1.reshape(b, s, D_MODEL), st0, g2, w["up"], w["conv_w"], w["conv_b"], w["down"], fg, final,
                        page_table, kcache_t, layer)
    return (y, kmean, a_k.reshape(b, s, H_A, HEAD_DIM), a_v.reshape(b, s, H_A, HEAD_DIM),
            mla_new.reshape(b, s, KV_LORA + D_ROPE), _ret_state_blocks(sfull), _chunks_to_state(cn))


def _sample_layer(x, w, g1, gq, gkv, gret, g2, fg, rel_bias, tabs, final,
                  kmean, kcache_t, vcache_t, mcache_t, layer, ret_state, conv_state, page_table, past_len):
    b = x.shape[0]
    a_q, a_k, a_v, r, mla_new, kcat, qcat = _in_proj(x, g1, w["in"], gq, gkv, w["uq_nope"], w["uq_rope"], w["uk_bd"],
                                                     tabs["mla_cos_s"], tabs["mla_sin_s"], 1)
    nbp = past_len // MOBA_BLOCK
    sel = _moba_select(a_q, kmean, nbp)
    o_a = _moba_decode(page_table, sel, a_q, a_k, a_v, kcache_t, vcache_t, layer, rel_bias, past_len)
    o_b, ret_new = _retention_step(r, ret_state, past_len, gret)
    o_lat = _mla_decode(page_table, qcat, kcat, mcache_t, layer)
    o_c = _matmul(o_lat, w["uv_exp"].reshape(H_C * KV_LORA, A_W))
    x1 = _out_proj(x, o_a, o_b, o_c, w["o_a"], w["o_b"], w["o_c"])
    nch = w["down"].shape[0]
    st = _state_to_chunks(conv_state, nch).transpose(2, 1, 3, 0, 4)
    y, cn = _ffn_step(x1, st, g2, w["up"], w["conv_w"], w["conv_b"], w["down"], fg, final)
    conv_new = _chunks_to_state(cn.transpose(3, 1, 0, 2, 4))
    return (y, a_k.reshape(b, 1, H_A, HEAD_DIM), a_v.reshape(b, 1, H_A, HEAD_DIM),
            mla_new.reshape(b, 1, KV_LORA + D_ROPE), ret_new, conv_new)


def kernel(x_prompt, x_sample, cache_moba_k, cache_moba_v, cache_mla, state_ret, state_conv, page_table, norm1_g, w_in, rel_bias, ret_norm_g, mla_q_norm_g, mla_kv_norm_g, w_uq, w_uk, w_uv, w_o, norm2_g, w_up, conv_w, conv_b, w_down, final_norm_g):
    depth = w_in.shape[0]
    bp, s, _ = x_prompt.shape
    bs, s_dec, _ = x_sample.shape
    n_pool, page = cache_moba_k.shape[1], cache_moba_k.shape[2]
    past_len = page_table.shape[1] * page
    assert s % MOBA_BLOCK == 0 and s_dec == 1 and MOBA_BLOCK % page == 0
    assert past_len % MOBA_BLOCK == 0 and past_len // MOBA_BLOCK >= MOBA_TOPK
    assert w_down.shape[1] % FFN_CH == 0

    pos_p = jnp.arange(s)
    pos_s = jnp.full((bs,), past_len)
    tabs = {}
    tabs["ret_cos"], tabs["ret_sin"] = _rope_tables(pos_p, RET_ROPE_BASE, HEAD_DIM, R_W)
    tabs["mla_cos"], tabs["mla_sin"] = _rope_tables(pos_p, MLA_ROPE_BASE, D_ROPE, LANES)
    tabs["mla_cos_s"], tabs["mla_sin_s"] = _rope_tables(pos_s, MLA_ROPE_BASE, D_ROPE, LANES)
    btab = _moba_bias_tiles(rel_bias)
    fg = final_norm_g.reshape(1, D_MODEL)

    kcache_t = jnp.transpose(cache_moba_k, (0, 1, 3, 4, 2))
    vcache_t = jnp.transpose(cache_moba_v, (0, 1, 3, 4, 2))
    mcache_t = jnp.transpose(cache_mla, (0, 1, 3, 2))

    xp = x_prompt
    xs = x_sample.reshape(bs, D_MODEL)
    outs_p, outs_s = [], []
    for l in range(depth):
        final = l == depth - 1
        w = _layer_weights(w_in[l], w_uq[l], w_uk[l], w_uv[l], w_o[l], w_up[l], conv_w[l], conv_b[l], w_down[l])
        norms = (norm1_g[l].reshape(1, -1), mla_q_norm_g[l].reshape(1, -1), mla_kv_norm_g[l].reshape(1, -1),
                 ret_norm_g[l].reshape(1, -1), norm2_g[l].reshape(1, -1), fg)
        xp, kmean, *rest_p = _prompt_layer(xp, w, *norms, btab, tabs, final, page_table, kcache_t, l)
        outs_p.append(rest_p)
        xs, *rest_s = _sample_layer(
            xs, w, *norms, rel_bias, tabs, final,
            kmean, kcache_t, vcache_t, mcache_t, l, state_ret[l], state_conv[l], page_table, past_len)
        outs_s.append(rest_s)
    stack = lambda outs, i: jnp.stack([o[i] for o in outs])
    return (xp, xs.reshape(bs, 1, D_MODEL),
            stack(outs_p, 0), stack(outs_p, 1), stack(outs_p, 2), stack(outs_p, 3), stack(outs_p, 4),
            stack(outs_s, 0), stack(outs_s, 1), stack(outs_s, 2), stack(outs_s, 3), stack(outs_s, 4))
```
